```python
import jax, jax.numpy as jnp
from jax import lax
import numpy as np

D_MODEL = 2048
BATCH = 4
SEQ = 4096
DEPTH = 2

EPS = 1e-6
SSD_WIDTH = D_MODEL // 2
SSD_HEAD_DIM = 64
SSD_HEADS = SSD_WIDTH // SSD_HEAD_DIM
SSD_GROUPS = 2
SSD_STATE = 128
SSD_CONV = 4
SSD_CHUNK = 128
ATT_WIDTH = D_MODEL - SSD_WIDTH
ATT_HEAD_DIM = 128
ATT_HEADS = ATT_WIDTH // ATT_HEAD_DIM
ATT_KV_HEADS = 4
IDX_HEADS = 16
IDX_HEAD_DIM = 64
IDX_TOPK_MAX = 256
Q_BLOCK = 128
D_FF = 4 * D_MODEL
MIX_WIDTH = SSD_WIDTH + ATT_WIDTH
DT_COL_SCALE = 0.1

N_Z = SSD_WIDTH
N_BC = SSD_GROUPS * SSD_STATE
CONV_CH = SSD_WIDTH + 2 * N_BC
N_DT = SSD_HEADS
N_Q = ATT_HEADS * ATT_HEAD_DIM
N_KV = ATT_KV_HEADS * ATT_HEAD_DIM
N_QI = IDX_HEADS * IDX_HEAD_DIM
N_KI = IDX_HEAD_DIM
N_WI = IDX_HEADS
IN_SIZES = (N_Z, CONV_CH, N_DT, N_Q, N_KV, N_KV, N_QI, N_KI, N_WI)
IN_WIDTH = N_Z + CONV_CH + N_DT + N_Q + 2 * N_KV + N_QI + N_KI + N_WI

kernel_name = "hymba_ssd_dsa_hybrid"


def _split(t, sizes):
    pts = [int(v) for v in np.cumsum(sizes)[:-1]]
    return jnp.split(t, pts, axis=-1)


def rmsnorm(x, g):
    xf = x.astype(jnp.float32)
    y = xf * lax.rsqrt(jnp.mean(xf * xf, axis=-1, keepdims=True) + EPS)
    return (y * g.astype(jnp.float32)).astype(x.dtype)


def causal_dwconv(u, w, bias):
    ch = u.shape[-1]
    out = lax.conv_general_dilated(
        u, w[:, None, :], window_strides=(1,), padding=((SSD_CONV - 1, 0),),
        dimension_numbers=("NWC", "WIO", "NWC"), feature_group_count=ch)
    return out + bias


def ssd_chunked(xs, dt, a, bm, cm):
    b, l, h, p = xs.shape
    g, n = bm.shape[-2:]
    r = h // g
    q = SSD_CHUNK
    c = l // q
    xdt = (xs * dt[..., None]).reshape(b, c, q, g, r, p)
    adt = (dt * a).reshape(b, c, q, g, r)
    bm = bm.reshape(b, c, q, g, n)
    cm = cm.reshape(b, c, q, g, n)
    a_cum = jnp.cumsum(adt, axis=2)
    seg = a_cum[:, :, :, None] - a_cum[:, :, None, :]
    causal = jnp.tril(jnp.ones((q, q), dtype=bool))[None, None, :, :, None, None]
    decay = jnp.exp(jnp.where(causal, seg, -jnp.inf))
    cb = jnp.einsum("bclgn,bcsgn->bclsg", cm, bm)
    y_diag = jnp.einsum("bclsg,bclsgr,bcsgrp->bclgrp", cb, decay, xdt)
    decay_state = jnp.exp(a_cum[:, :, -1:] - a_cum)
    states = jnp.einsum("bcsgn,bcsgr,bcsgrp->bcgrpn", bm, decay_state, xdt)
    chunk_decay = jnp.exp(a_cum[:, :, -1])

    def step(carry, inp):
        st, dec = inp
        return carry * dec[..., None, None] + st, carry

    init = jnp.zeros((b, g, r, p, n), dtype=xs.dtype)
    _, prev = lax.scan(step, init, (jnp.swapaxes(states, 0, 1), jnp.swapaxes(chunk_decay, 0, 1)))
    prev = jnp.swapaxes(prev, 0, 1)
    y_off = jnp.einsum("bclgn,bcgrpn,bclgr->bclgrp", cm, prev, jnp.exp(a_cum))
    return (y_diag + y_off).reshape(b, l, h, p)


def dsa_attention(q, k, v, q_idx, k_idx, w_idx):
    b, l = q.shape[:2]
    topk = min(IDX_TOPK_MAX, l // 4)
    qb = min(Q_BLOCK, l)
    nb = l // qb
    rep = ATT_HEADS // ATT_KV_HEADS
    key_pos = jnp.arange(l)

    def to_blocks(t):
        return jnp.swapaxes(t.reshape(b, nb, qb, *t.shape[2:]), 0, 1)

    def one_block(args):
        q_b, qi_b, wi_b, start = args
        t_pos = start + jnp.arange(qb)
        s = jnp.einsum("bqhd,bsd->bqhs", qi_b, k_idx) * (IDX_HEAD_DIM ** -0.5)
        score = jnp.einsum("bqhs,bqh->bqs", jax.nn.relu(s), wi_b).astype(jnp.float32)
        admissible = key_pos[None, :] <= t_pos[:, None]
        score = jnp.where(admissible[None], score, -jnp.inf)
        _, sel = lax.top_k(score, topk)
        sel_ok = sel <= t_pos[None, :, None]
        k_sel = jax.vmap(lambda kk, ii: kk[ii])(k, sel)
        v_sel = jax.vmap(lambda vv, ii: vv[ii])(v, sel)
        qg = q_b.reshape(b, qb, ATT_KV_HEADS, rep, ATT_HEAD_DIM)
        logits = jnp.einsum("bqhgd,bqkhd->bqhgk", qg, k_sel).astype(jnp.float32) * (ATT_HEAD_DIM ** -0.5)
        logits = jnp.where(sel_ok[:, :, None, None, :], logits, -jnp.inf)
        prob = jax.nn.softmax(logits, axis=-1).astype(v.dtype)
        o = jnp.einsum("bqhgk,bqkhd->bqhgd", prob, v_sel)
        return o.reshape(b, qb, ATT_HEADS * ATT_HEAD_DIM)

    starts = jnp.arange(nb, dtype=jnp.int32) * qb
    out = lax.map(one_block, (to_blocks(q), to_blocks(q_idx), to_blocks(w_idx), starts))
    return jnp.swapaxes(out, 0, 1).reshape(b, l, ATT_HEADS * ATT_HEAD_DIM)


def hybrid_layer(x, g_mix, w_in, conv_w, conv_b, dt_bias, a_log, d_skip, g_ssd,
                 g_q, g_k, g_kidx, w_out, g_mlp, w_up, w_down):
    b, l, _ = x.shape
    h = rmsnorm(x, g_mix) @ w_in
    z, xbc, dt, q, k, v, qi, ki, wi = _split(h, IN_SIZES)

    xbc = jax.nn.silu(causal_dwconv(xbc, conv_w, conv_b))
    xs, bm, cm = _split(xbc, (SSD_WIDTH, N_BC, N_BC))
    xs_f = xs.astype(jnp.float32).reshape(b, l, SSD_HEADS, SSD_HEAD_DIM)
    dt_f = jax.nn.softplus(dt.astype(jnp.float32) + dt_bias.astype(jnp.float32))
    a = -jnp.exp(a_log.astype(jnp.float32))
    y = ssd_chunked(xs_f, dt_f, a,
                    bm.astype(jnp.float32).reshape(b, l, SSD_GROUPS, SSD_STATE),
                    cm.astype(jnp.float32).reshape(b, l, SSD_GROUPS, SSD_STATE))
    y = y + d_skip.astype(jnp.float32)[:, None] * xs_f
    y = y.reshape(b, l, SSD_WIDTH) * jax.nn.silu(z.astype(jnp.float32))
    yg = y.reshape(b, l, SSD_GROUPS, SSD_WIDTH // SSD_GROUPS)
    yg = yg * lax.rsqrt(jnp.mean(yg * yg, axis=-1, keepdims=True) + EPS)
    y_ssd = (yg.reshape(b, l, SSD_WIDTH) * g_ssd.astype(jnp.float32)).astype(x.dtype)

    q = rmsnorm(q.reshape(b, l, ATT_HEADS, ATT_HEAD_DIM), g_q)
    k = rmsnorm(k.reshape(b, l, ATT_KV_HEADS, ATT_HEAD_DIM), g_k)
    v = v.reshape(b, l, ATT_KV_HEADS, ATT_HEAD_DIM)
    qi = qi.reshape(b, l, IDX_HEADS, IDX_HEAD_DIM)
    ki = rmsnorm(ki, g_kidx)
    wi = wi * (IDX_HEADS ** -0.5)
    y_att = dsa_attention(q, k, v, qi, ki, wi)

    x = x + jnp.concatenate([y_ssd, y_att], axis=-1) @ w_out

    u = rmsnorm(x, g_mlp) @ w_up
    return x + jnp.square(jax.nn.relu(u)) @ w_down


def setup_inputs(seed: int = 0) -> dict:
    key = jax.random.key(seed)
    ks = jax.random.split(key, 20)
    f32 = jnp.float32
    nrm = lambda k, shape, s: jax.random.normal(k, shape, f32) * s
    gain = lambda k, shape: 1.0 + 0.02 * jax.random.normal(k, shape, f32)

    x = jax.random.normal(ks[0], (BATCH, SEQ, D_MODEL), f32)
    col_scale = jnp.ones((IN_WIDTH,), f32).at[N_Z + CONV_CH:N_Z + CONV_CH + N_DT].set(DT_COL_SCALE)
    w_in = nrm(ks[1], (DEPTH, D_MODEL, IN_WIDTH), D_MODEL ** -0.5) * col_scale
    conv_w = nrm(ks[2], (DEPTH, SSD_CONV, CONV_CH), SSD_CONV ** -0.5)
    conv_b = nrm(ks[3], (DEPTH, CONV_CH), 0.01)
    u = jax.random.uniform(ks[4], (DEPTH, SSD_HEADS), f32)
    dt0 = jnp.exp(u * (jnp.log(0.1) - jnp.log(0.001)) + jnp.log(0.001))
    dt_bias = dt0 + jnp.log(-jnp.expm1(-dt0))
    a_log = jnp.log(jax.random.uniform(ks[5], (DEPTH, SSD_HEADS), f32, 1.0, 16.0))
    d_skip = 1.0 + 0.1 * jax.random.normal(ks[6], (DEPTH, SSD_HEADS), f32)
    return {
        "x": x,
        "g_mix": gain(ks[7], (DEPTH, D_MODEL)),
        "w_in": w_in,
        "conv_w": conv_w,
        "conv_b": conv_b,
        "dt_bias": dt_bias,
        "a_log": a_log,
        "d_skip": d_skip,
        "g_ssd": gain(ks[8], (DEPTH, SSD_WIDTH)),
        "g_q": gain(ks[9], (DEPTH, ATT_HEAD_DIM)),
        "g_k": gain(ks[10], (DEPTH, ATT_HEAD_DIM)),
        "g_kidx": gain(ks[11], (DEPTH, IDX_HEAD_DIM)),
        "w_out": nrm(ks[12], (DEPTH, MIX_WIDTH, D_MODEL), MIX_WIDTH ** -0.5),
        "g_mlp": gain(ks[13], (DEPTH, D_MODEL)),
        "w_up": nrm(ks[14], (DEPTH, D_MODEL, D_FF), D_MODEL ** -0.5),
        "w_down": nrm(ks[15], (DEPTH, D_FF, D_MODEL), D_FF ** -0.5),
    }


def reference(x, g_mix, w_in, conv_w, conv_b, dt_bias, a_log, d_skip, g_ssd,
              g_q, g_k, g_kidx, w_out, g_mlp, w_up, w_down):
    for i in range(DEPTH):
        x = hybrid_layer(x, g_mix[i], w_in[i], conv_w[i], conv_b[i], dt_bias[i], a_log[i],
                         d_skip[i], g_ssd[i], g_q[i], g_k[i], g_kidx[i], w_out[i],
                         g_mlp[i], w_up[i], w_down[i])
    return x
```

```python
import functools

import numpy as np
import jax
import jax.numpy as jnp
from jax import lax
from jax.experimental import pallas as pl
from jax.experimental.pallas import tpu as pltpu

EPS = 1e-6
SSD_HEAD_DIM = 64
SSD_GROUPS = 2
SSD_STATE = 128
SSD_CONV = 4
SSD_CHUNK = 128
ATT_HEAD_DIM = 128
ATT_KV_HEADS = 4
IDX_HEADS = 16
IDX_HEAD_DIM = 64
IDX_TOPK_MAX = 256

LANES = 128
SUBLANES = 8
VMEM_LIMIT = 56 * 1024 * 1024

WI_LO, DT_LO = 0, 16
NEG_BIG = -1e30
INT_MIN = -(2 ** 31)
KEY_NEG_INF = 0x807FFFFF - 2 ** 32
REFINE_STEPS = 30
LOG2_E = 1.4426950408889634

F32 = jnp.float32
BF16 = jnp.bfloat16
HIGHEST = lax.Precision.HIGHEST


def _dims(d_model, seq):
    ssd_w = d_model // 2
    att_w = d_model - ssd_w
    n_bc = SSD_GROUPS * SSD_STATE
    return dict(
        ssd_w=ssd_w, ssd_heads=ssd_w // SSD_HEAD_DIM, n_bc=n_bc, conv_ch=ssd_w + 2 * n_bc,
        n_q=att_w, att_heads=att_w // ATT_HEAD_DIM, n_kv=ATT_KV_HEADS * ATT_HEAD_DIM,
        n_qi=IDX_HEADS * IDX_HEAD_DIM, topk=min(IDX_TOPK_MAX, seq // 4))


def _nt_dot(a, b):
    return lax.dot_general(a, b, (((1,), (1,)), ((), ())), preferred_element_type=F32)


def _silu(v):
    return v * (1.0 / (1.0 + jnp.exp(-v)))


def _softplus(v):
    return jnp.maximum(v, 0.0) + jnp.log1p(jnp.exp(-jnp.abs(v)))


def _in_proj_kernel(x_ref, gmix_ref, w_ref, gq_ref, gk_ref, gki_ref, dtb_ref,
                    z_ref, xbc_ref, q_ref, k_ref, vt_ref, qi_ref, kia_ref, kib_ref, sm_ref, *, dm):
    x = x_ref[...]
    ms = jnp.mean(x * x, axis=-1, keepdims=True)
    xn = (x * lax.rsqrt(ms + EPS) * gmix_ref[...]).astype(BF16)

    off = [0]

    def seg(width):
        lo = off[0]
        off[0] = lo + width
        return jnp.dot(xn, w_ref[:, lo:lo + width], preferred_element_type=F32)

    z_ref[...] = seg(dm["ssd_w"])
    xbc_ref[...] = seg(dm["conv_ch"])

    def head_norm(h, gain, n_heads, out_ref):
        for i in range(n_heads):
            hh = h[:, i * ATT_HEAD_DIM:(i + 1) * ATT_HEAD_DIM]
            m2 = jnp.mean(hh * hh, axis=-1, keepdims=True)
            out_ref[:, i * ATT_HEAD_DIM:(i + 1) * ATT_HEAD_DIM] = (hh * lax.rsqrt(m2 + EPS) * gain).astype(out_ref.dtype)

    head_norm(seg(dm["n_q"]), gq_ref[...] * (ATT_HEAD_DIM ** -0.5 * LOG2_E), dm["att_heads"], q_ref)
    head_norm(seg(dm["n_kv"]), gk_ref[...], ATT_KV_HEADS, k_ref)
    vt_ref[0] = seg(dm["n_kv"]).T.astype(BF16)
    qi_ref[...] = (seg(dm["n_qi"]) * (IDX_HEAD_DIM ** -0.5)).astype(BF16)

    lane = lax.broadcasted_iota(jnp.int32, (1, LANES), 1)
    kk = seg(LANES)
    m2 = jnp.mean(kk * kk, axis=-1, keepdims=True)
    kin = kk * lax.rsqrt(m2 + EPS) * gki_ref[...]
    kia_ref[...] = jnp.where(lane < IDX_HEAD_DIM, kin, 0.0).astype(BF16)
    kib_ref[...] = jnp.where(lane >= IDX_HEAD_DIM, kin, 0.0).astype(BF16)

    sm = seg(LANES)
    wi = sm * (IDX_HEADS ** -0.5)
    dt = _softplus(sm + dtb_ref[...])
    sm_ref[...] = jnp.where(lane < DT_LO, wi, jnp.where(lane < DT_LO + dm["ssd_heads"], dt, 0.0))


def _in_proj(x2, gmix, w_all, gq, gk, gki2, dtb, *, dm, tm):
    t, d = x2.shape
    nw = w_all.shape[1]
    n_kv = dm["n_kv"]
    row = lambda width: pl.BlockSpec((tm, width), lambda i: (i, 0))
    const = lambda shape: pl.BlockSpec(shape, lambda i: (0,) * len(shape))
    out_shape = (
        jax.ShapeDtypeStruct((t, dm["ssd_w"]), F32),
        jax.ShapeDtypeStruct((t, dm["conv_ch"]), F32),
        jax.ShapeDtypeStruct((t, dm["n_q"]), BF16),
        jax.ShapeDtypeStruct((t, n_kv), BF16),
        jax.ShapeDtypeStruct((t // tm, n_kv, tm), BF16),
        jax.ShapeDtypeStruct((t, dm["n_qi"]), BF16),
        jax.ShapeDtypeStruct((t, LANES), BF16),
        jax.ShapeDtypeStruct((t, LANES), BF16),
        jax.ShapeDtypeStruct((t, LANES), F32),
    )
    out_specs = (
        row(dm["ssd_w"]), row(dm["conv_ch"]), row(dm["n_q"]), row(n_kv),
        pl.BlockSpec((1, n_kv, tm), lambda i: (i, 0, 0)),
        row(dm["n_qi"]), row(LANES), row(LANES), row(LANES),
    )
    return pl.pallas_call(
        functools.partial(_in_proj_kernel, dm=dm),
        grid=(t // tm,),
        in_specs=[
            row(d), const((1, d)),
            pl.BlockSpec((d, nw), lambda i: (0, 0), pipeline_mode=pl.Buffered(1)),
            const((1, ATT_HEAD_DIM)), const((1, ATT_HEAD_DIM)), const((1, LANES)), const((1, LANES)),
        ],
        out_specs=out_specs,
        out_shape=out_shape,
        compiler_params=pltpu.CompilerParams(dimension_semantics=("arbitrary",), vmem_limit_bytes=VMEM_LIMIT),
        name="in_proj",
    )(x2, gmix, w_all, gq, gk, gki2, dtb)


def _ssd_kernel(xbc_ref, z_ref, sm_ref, cw_ref, cb_ref, alog_ref, dskip_ref, gssd_ref, e_ref,
                y_ref, ubuf_ref, state_ref, *, dm):
    q = SSD_CHUNK
    ssd_w = dm["ssd_w"]
    n_bc = dm["n_bc"]
    gw = ssd_w // SSD_GROUPS
    hpg = dm["ssd_heads"] // SSD_GROUPS

    @pl.when(pl.program_id(1) == 0)
    def _():
        ubuf_ref[0:SUBLANES, :] = jnp.zeros((SUBLANES, ubuf_ref.shape[1]), F32)
        state_ref[...] = jnp.zeros_like(state_ref)

    u = xbc_ref[...]
    ubuf_ref[SUBLANES:SUBLANES + q, :] = u
    acc = cb_ref[...] + cw_ref[SSD_CONV - 1:SSD_CONV, :] * u
    for s in range(1, SSD_CONV):
        acc = acc + cw_ref[SSD_CONV - 1 - s:SSD_CONV - s, :] * ubuf_ref[SUBLANES - s:SUBLANES - s + q, :]
    ubuf_ref[0:SUBLANES, :] = ubuf_ref[q:q + SUBLANES, :]
    xc = _silu(acc)
    xs = xc[:, :ssd_w]
    bm = xc[:, ssd_w:ssd_w + n_bc]
    cm = xc[:, ssd_w + n_bc:ssd_w + 2 * n_bc]

    lane = lax.broadcasted_iota(jnp.int32, (1, LANES), 1)
    is_dt = (lane >= DT_LO) & (lane < DT_LO + dm["ssd_heads"])
    dt_full = jnp.where(is_dt, sm_ref[...], 0.0)
    a_full = jnp.where(is_dt, -jnp.exp(alog_ref[...]), 0.0)
    adt = dt_full * a_full

    row_i = lax.broadcasted_iota(jnp.int32, (q, q), 0)
    col_i = lax.broadcasted_iota(jnp.int32, (q, q), 1)
    causal = row_i >= col_i
    tri = causal.astype(F32)
    a_cum = jnp.dot(tri, adt, precision=HIGHEST, preferred_element_type=F32)
    a_cum_t = a_cum.T
    e = e_ref[...]
    acum_x = jnp.dot(a_cum, e, precision=HIGHEST, preferred_element_type=F32)
    dt_x = jnp.dot(dt_full, e, precision=HIGHEST, preferred_element_type=F32)
    alast_x = acum_x[q - 1:q, :]
    decay_in = jnp.exp(acum_x)
    decay_out = jnp.exp(alast_x - acum_x)
    chunk_decay = jnp.exp(alast_x)

    xdt = xs * dt_x
    xdt_b = xdt.astype(BF16)
    xw_b = (xdt * decay_out).astype(BF16)

    y_diag, y_off = [], []
    for g in range(SSD_GROUPS):
        cm_g = cm[:, g * SSD_STATE:(g + 1) * SSD_STATE]
        bm_g = bm[:, g * SSD_STATE:(g + 1) * SSD_STATE]
        cm_b = cm_g.astype(BF16)
        cb = _nt_dot(cm_b, bm_g.astype(BF16))
        s_prev = state_ref[:, g * gw:(g + 1) * gw]
        y_off.append(jnp.dot(cm_b, s_prev.astype(BF16), preferred_element_type=F32))
        state_ref[:, g * gw:(g + 1) * gw] = (
            s_prev * chunk_decay[:, g * gw:(g + 1) * gw]
            + jnp.dot(bm_g.T.astype(BF16), xw_b[:, g * gw:(g + 1) * gw], preferred_element_type=F32))
        for j in range(hpg // 2):
            h0 = g * hpg + 2 * j
            xp = xdt_b[:, h0 * SSD_HEAD_DIM:(h0 + 2) * SSD_HEAD_DIM]
            pair = None
            for r in range(2):
                c = DT_LO + h0 + r
                seg = a_cum[:, c:c + 1] - a_cum_t[c:c + 1, :]
                decay = jnp.exp(jnp.where(causal, seg, -jnp.inf))
                m = (cb * decay).astype(BF16)
                keep = (lane < SSD_HEAD_DIM) if r == 0 else (lane >= SSD_HEAD_DIM)
                part = jnp.dot(m, jnp.where(keep, xp, jnp.zeros_like(xp)), preferred_element_type=F32)
                pair = part if pair is None else pair + part
            y_diag.append(pair)

    y = jnp.concatenate(y_diag, axis=1) + decay_in * jnp.concatenate(y_off, axis=1) + dskip_ref[...] * xs
    y = y * _silu(z_ref[...])
    outs = []
    for g in range(SSD_GROUPS):
        yg = y[:, g * gw:(g + 1) * gw]
        outs.append(yg * lax.rsqrt(jnp.mean(yg * yg, axis=-1, keepdims=True) + EPS))
    y_ref[...] = (jnp.concatenate(outs, axis=1) * gssd_ref[...]).astype(y_ref.dtype)


def _ssd(xbc, z, sm, conv_w, conv_b, alog_pad, dskip_x, g_ssd, e_mat, *, dm, batch, seq):
    t = xbc.shape[0]
    nc = seq // SSD_CHUNK
    cc = dm["conv_ch"]
    ssd_w = dm["ssd_w"]
    row = lambda width: pl.BlockSpec((SSD_CHUNK, width), lambda b, c: (b * nc + c, 0))
    const = lambda shape: pl.BlockSpec(shape, lambda b, c: (0,) * len(shape))
    return pl.pallas_call(
        functools.partial(_ssd_kernel, dm=dm),
        grid=(batch, nc),
        in_specs=[row(cc), row(ssd_w), row(LANES),
                  const((SSD_CONV, cc)), const((1, cc)), const((1, LANES)), const((1, ssd_w)), const((1, ssd_w)),
                  const((LANES, ssd_w))],
        out_specs=row(ssd_w),
        out_shape=jax.ShapeDtypeStruct((t, ssd_w), BF16),
        scratch_shapes=[pltpu.VMEM((SSD_CHUNK + SUBLANES, cc), F32), pltpu.VMEM((SSD_STATE, ssd_w), F32)],
        compiler_params=pltpu.CompilerParams(dimension_semantics=("arbitrary", "arbitrary"), vmem_limit_bytes=VMEM_LIMIT),
        name="ssd",
    )(xbc, z, sm, conv_w, conv_b, alog_pad, dskip_x, g_ssd, e_mat)


def _key_to_f32(ks):
    return lax.bitcast_convert_type(ks ^ ((ks >> 31) & jnp.int32(0x7FFFFFFF)), F32)


COUNT_ACCS = 4


def _count_rows(m, cnt):
    cnt = list(cnt)
    for j, r in enumerate(range(0, m.shape[0], SUBLANES)):
        a = j % COUNT_ACCS
        cnt[a] = jnp.where(m[r:r + SUBLANES, :], cnt[a] + 1, cnt[a])
    return tuple(cnt)


def _dsa_kernel(q_ref, k_ref, vt_ref, qi_ref, kia_ref, kib_ref, sm_ref, o_ref,
                score_ref, bias_ref, thr_ref, mstar_ref, m_ref, s_ref, acc_ref, logit_ref, *, dm, seq, qb):
    kc = qb
    topk = dm["topk"]
    n_heads = dm["att_heads"]
    rep = n_heads // ATT_KV_HEADS
    i = pl.program_id(1)
    nkc = i + 1
    q_pos = i * qb + lax.broadcasted_iota(jnp.int32, (1, qb), 1)

    def key_pos(c):
        return c * kc + lax.broadcasted_iota(jnp.int32, (kc, 1), 0)

    def rows(c):
        return pl.ds(pl.multiple_of(c * kc, kc), kc)

    wi_t = sm_ref[...].T
    qi = qi_ref[...]

    def score_chunk(c, carry):
        ka = kia_ref[rows(c), :]
        kb = kib_ref[rows(c), :]
        acc = jnp.zeros((kc, qb), F32)
        for j in range(IDX_HEADS // 2):
            qp = qi[:, j * LANES:(j + 1) * LANES]
            w0 = wi_t[WI_LO + 2 * j:WI_LO + 2 * j + 1, :]
            w1 = wi_t[WI_LO + 2 * j + 1:WI_LO + 2 * j + 2, :]
            acc = acc + w0 * jnp.maximum(_nt_dot(ka, qp), 0.0) + w1 * jnp.maximum(_nt_dot(kb, qp), 0.0)
        score_ref[rows(c), :] = jnp.where(key_pos(c) <= q_pos, acc, -jnp.inf)
        return carry

    lax.fori_loop(0, nkc, score_chunk, 0)

    def count(pred):
        def body(c, cnt):
            return _count_rows(pred(score_ref[rows(c), :], c), cnt)
        cnt = lax.fori_loop(0, nkc, body, (jnp.zeros((SUBLANES, qb), jnp.int32),) * COUNT_ACCS)
        return jnp.sum(functools.reduce(lambda a, b: a + b, cnt), axis=0, keepdims=True)

    def bit_step(p, res):
        cand = res | (jnp.int32(1) << (31 - p))
        cand_s = cand ^ jnp.int32(INT_MIN)
        cand_f = _key_to_f32(cand_s)
        n_ge = count(lambda sc, c: sc >= cand_f)
        return jnp.where((n_ge >= topk) | (cand_s <= jnp.int32(KEY_NEG_INF)), cand, res)

    thr_s = lax.fori_loop(0, 32, bit_step, jnp.zeros((1, qb), jnp.int32)) ^ jnp.int32(INT_MIN)
    thr0 = _key_to_f32(thr_s)
    thr_ref[...] = thr0
    mstar_ref[...] = jnp.full((1, qb), seq, jnp.int32)
    n_ge = count(lambda sc, c: sc >= thr0)

    @pl.when(jnp.max(n_ge) > topk)
    def _():
        def refine(_, lh):
            lo, hi = lh
            mid = lo + 0.5 * (hi - lo)
            ok = count(lambda sc, c: sc >= mid) >= topk
            return jnp.where(ok, mid, lo), jnp.where(ok, hi, mid)

        thr, _ = lax.fori_loop(0, REFINE_STEPS, refine, (thr0, _key_to_f32(thr_s + 1)))
        thr_ref[...] = thr
        need = topk - count(lambda sc, c: sc > thr)
        nbits = max(1, int(np.ceil(np.log2(seq))))

        def idx_step(p, res):
            cand = res | (jnp.int32(1) << (nbits - 1 - p))
            n_eq_before = count(lambda sc, c: (sc == thr) & (key_pos(c) < cand))
            return jnp.where(n_eq_before < need, cand, res)

        mstar_ref[...] = lax.fori_loop(0, nbits, idx_step, jnp.zeros((1, qb), jnp.int32))

    thr = thr_ref[...]
    mstar = mstar_ref[...]

    def bias_chunk(c, carry):
        sc = score_ref[rows(c), :]
        kp = key_pos(c)
        sel = ((sc > thr) | ((sc == thr) & (kp <= mstar))) & (kp <= q_pos)
        bias_ref[rows(c), :] = jnp.where(sel, 0.0, NEG_BIG)
        return carry

    lax.fori_loop(0, nkc, bias_chunk, 0)

    m_ref[...] = jnp.full(m_ref.shape, NEG_BIG, F32)
    s_ref[...] = jnp.zeros(s_ref.shape, F32)
    acc_ref[...] = jnp.zeros(acc_ref.shape, F32)

    def att_chunk(c, carry):
        bias = bias_ref[rows(c), :]
        chunk_max = []
        for h in range(n_heads):
            g = h // rep
            kch = k_ref[rows(c), g * ATT_HEAD_DIM:(g + 1) * ATT_HEAD_DIM]
            logit = _nt_dot(kch, q_ref[:, h * ATT_HEAD_DIM:(h + 1) * ATT_HEAD_DIM]) + bias
            logit_ref[h] = logit
            chunk_max.append(jnp.max(logit, axis=0, keepdims=True))
        for h in range(n_heads):
            g = h // rep
            vt = vt_ref[c, g * ATT_HEAD_DIM:(g + 1) * ATT_HEAD_DIM, :]
            m_old = m_ref[h:h + 1, :]
            m_new = jnp.maximum(m_old, chunk_max[h])
            alpha = jnp.exp2(m_old - m_new)
            p = jnp.exp2(logit_ref[h] - m_new)
            m_ref[h:h + 1, :] = m_new
            s_ref[h:h + 1, :] = s_ref[h:h + 1, :] * alpha + jnp.sum(p, axis=0, keepdims=True)
            acc_ref[h] = acc_ref[h] * alpha + jnp.dot(vt, p.astype(BF16), preferred_element_type=F32)
        return carry

    lax.fori_loop(0, nkc, att_chunk, 0)
    for h in range(n_heads):
        o_ref[:, h * ATT_HEAD_DIM:(h + 1) * ATT_HEAD_DIM] = (acc_ref[h] / s_ref[h:h + 1, :]).T.astype(o_ref.dtype)


def _dsa(q, k, vt, qi, kia, kib, sm, *, dm, batch, seq, qb):
    t = q.shape[0]
    nqb = seq // qb
    n_kv = dm["n_kv"]
    assert qb >= dm["topk"] and seq % qb == 0
    qrow = lambda width: pl.BlockSpec((qb, width), lambda b, i: (b * nqb + i, 0))
    brow = lambda width: pl.BlockSpec((seq, width), lambda b, i: (b, 0))
    return pl.pallas_call(
        functools.partial(_dsa_kernel, dm=dm, seq=seq, qb=qb),
        grid=(batch, nqb),
        in_specs=[qrow(dm["n_q"]), brow(n_kv),
                  pl.BlockSpec((nqb, n_kv, qb), lambda b, i: (b, 0, 0)),
                  qrow(dm["n_qi"]), brow(LANES), brow(LANES), qrow(LANES)],
        out_specs=qrow(dm["n_q"]),
        out_shape=jax.ShapeDtypeStruct((t, dm["n_q"]), BF16),
        scratch_shapes=[pltpu.VMEM((seq, qb), F32), pltpu.VMEM((seq, qb), F32),
                        pltpu.VMEM((1, qb), F32), pltpu.VMEM((1, qb), jnp.int32),
                        pltpu.VMEM((dm["att_heads"], qb), F32), pltpu.VMEM((dm["att_heads"], qb), F32),
                        pltpu.VMEM((dm["att_heads"], ATT_HEAD_DIM, qb), F32),
                        pltpu.VMEM((dm["att_heads"], qb, qb), F32)],
        compiler_params=pltpu.CompilerParams(dimension_semantics=("arbitrary", "arbitrary"), vmem_limit_bytes=VMEM_LIMIT),
        name="dsa",
    )(q, k, vt, qi, kia, kib, sm)


def _out_proj_kernel(x_ref, ya_ref, yb_ref, w_ref, o_ref):
    na = ya_ref.shape[1]
    o_ref[...] = (x_ref[...]
                  + jnp.dot(ya_ref[...], w_ref[:na, :], preferred_element_type=F32)
                  + jnp.dot(yb_ref[...], w_ref[na:, :], preferred_element_type=F32))


def _out_proj(x2, y_ssd, y_att, w_out, *, tm):
    t, d = x2.shape
    row = lambda width: pl.BlockSpec((tm, width), lambda i: (i, 0))
    return pl.pallas_call(
        _out_proj_kernel,
        grid=(t // tm,),
        in_specs=[row(d), row(y_ssd.shape[1]), row(y_att.shape[1]),
                  pl.BlockSpec(w_out.shape, lambda i: (0, 0), pipeline_mode=pl.Buffered(1))],
        out_specs=row(d),
        out_shape=jax.ShapeDtypeStruct((t, d), F32),
        compiler_params=pltpu.CompilerParams(dimension_semantics=("arbitrary",), vmem_limit_bytes=VMEM_LIMIT),
        name="out_proj",
    )(x2, y_ssd, y_att, w_out)


def _mlp_kernel(x_ref, g_ref, wu_ref, wd_ref, o_ref, xn_ref, acc_ref):
    f = pl.program_id(1)

    @pl.when(f == 0)
    def _():
        x = x_ref[...]
        ms = jnp.mean(x * x, axis=-1, keepdims=True)
        xn_ref[...] = (x * lax.rsqrt(ms + EPS) * g_ref[...]).astype(BF16)
        acc_ref[...] = jnp.zeros_like(acc_ref)

    u = jnp.maximum(jnp.dot(xn_ref[...], wu_ref[...], preferred_element_type=F32), 0.0)
    acc_ref[...] += jnp.dot((u * u).astype(BF16), wd_ref[...], preferred_element_type=F32)

    @pl.when(f == pl.num_programs(1) - 1)
    def _():
        o_ref[...] = x_ref[...] + acc_ref[...]


def _mlp(x2, g_mlp, w_up, w_down, *, tm, tf):
    t, d = x2.shape
    d_ff = w_up.shape[1]
    return pl.pallas_call(
        _mlp_kernel,
        grid=(t // tm, d_ff // tf),
        in_specs=[pl.BlockSpec((tm, d), lambda i, f: (i, 0)),
                  pl.BlockSpec((1, d), lambda i, f: (0, 0)),
                  pl.BlockSpec((d, tf), lambda i, f: (0, f)),
                  pl.BlockSpec((tf, d), lambda i, f: (f, 0))],
        out_specs=pl.BlockSpec((tm, d), lambda i, f: (i, 0)),
        out_shape=jax.ShapeDtypeStruct((t, d), F32),
        scratch_shapes=[pltpu.VMEM((tm, d), BF16), pltpu.VMEM((tm, d), F32)],
        compiler_params=pltpu.CompilerParams(dimension_semantics=("arbitrary", "arbitrary"), vmem_limit_bytes=VMEM_LIMIT),
        name="mlp",
    )(x2, g_mlp, w_up, w_down)


def _pack_w_in(w_in, dm):
    sizes = (dm["ssd_w"], dm["conv_ch"], dm["ssd_heads"], dm["n_q"], dm["n_kv"], dm["n_kv"], dm["n_qi"],
             IDX_HEAD_DIM, IDX_HEADS)
    pts = [int(v) for v in np.cumsum(sizes)[:-1]]
    z, xbc, dt, q, k, v, qi, ki, wi = jnp.split(w_in, pts, axis=1)
    pad = jnp.zeros((w_in.shape[0], LANES - IDX_HEADS - dm["ssd_heads"]), w_in.dtype)
    return jnp.concatenate([z, xbc, q, k, v, qi, ki, ki, wi, dt, pad], axis=1).astype(BF16)


def _expand_matrix(dm):
    e = np.zeros((LANES, dm["ssd_w"]), np.float32)
    for h in range(dm["ssd_heads"]):
        e[DT_LO + h, h * SSD_HEAD_DIM:(h + 1) * SSD_HEAD_DIM] = 1.0
    return jnp.asarray(e)


def _lane_pad(v, lo):
    return jnp.zeros((1, LANES), F32).at[0, lo:lo + v.shape[0]].set(v)


def kernel(x, g_mix, w_in, conv_w, conv_b, dt_bias, a_log, d_skip, g_ssd, g_q, g_k, g_kidx, w_out, g_mlp, w_up, w_down):
    batch, seq, d = x.shape
    depth = w_in.shape[0]
    dm = _dims(d, seq)
    assert dm["ssd_heads"] + IDX_HEADS <= LANES and (dm["ssd_heads"] // SSD_GROUPS) % 2 == 0
    assert seq % SSD_CHUNK == 0 and dm["att_heads"] % ATT_KV_HEADS == 0
    qb = 256
    tm = min(512, batch * seq)
    tf = min(512, w_up.shape[2])
    e_mat = _expand_matrix(dm)

    x2 = x.reshape(batch * seq, d)
    for i in range(depth):
        w_all = _pack_w_in(w_in[i], dm)
        z, xbc, q, k, vt, qi, kia, kib, sm = _in_proj(
            x2, g_mix[i][None, :], w_all, g_q[i][None, :], g_k[i][None, :],
            jnp.concatenate([g_kidx[i], g_kidx[i]])[None, :], _lane_pad(dt_bias[i], DT_LO), dm=dm, tm=qb)
        y_ssd = _ssd(xbc, z, sm, conv_w[i], conv_b[i][None, :], _lane_pad(a_log[i], DT_LO),
                     jnp.repeat(d_skip[i], SSD_HEAD_DIM)[None, :], g_ssd[i][None, :], e_mat,
                     dm=dm, batch=batch, seq=seq)
        y_att = _dsa(q, k, vt, qi, kia, kib, sm, dm=dm, batch=batch, seq=seq, qb=qb)
        x2 = _out_proj(x2, y_ssd, y_att, w_out[i].astype(BF16), tm=tm)
        x2 = _mlp(x2, g_mlp[i][None, :], w_up[i].astype(BF16), w_down[i].astype(BF16), tm=tm, tf=tf)
    return x2.reshape(batch, seq, d)
```

```python
import functools

import numpy as np
import jax
import jax.numpy as jnp
from jax import lax
from jax.experimental import pallas as pl
from jax.experimental.pallas import tpu as pltpu

EPS = 1e-6
SSD_HEAD_DIM = 64
SSD_GROUPS = 2
SSD_STATE = 128
SSD_CONV = 4
SSD_CHUNK = 128
ATT_HEAD_DIM = 128
ATT_KV_HEADS = 4
IDX_HEADS = 16
IDX_HEAD_DIM = 64
IDX_TOPK_MAX = 256

LANES = 128
SUBLANES = 8
VMEM_LIMIT = 56 * 1024 * 1024

WI_LO, DT_LO = 0, 16
NEG_BIG = -1e30
INT_MIN = -(2 ** 31)
KEY_NEG_INF = 0x807FFFFF - 2 ** 32
REFINE_STEPS = 30
LOG2_E = 1.4426950408889634
SUM_ROWS = 16

F32 = jnp.float32
BF16 = jnp.bfloat16
HIGHEST = lax.Precision.HIGHEST


def _dims(d_model, seq):
    ssd_w = d_model // 2
    att_w = d_model - ssd_w
    n_bc = SSD_GROUPS * SSD_STATE
    return dict(
        ssd_w=ssd_w, ssd_heads=ssd_w // SSD_HEAD_DIM, n_bc=n_bc, conv_ch=ssd_w + 2 * n_bc,
        n_q=att_w, att_heads=att_w // ATT_HEAD_DIM, n_kv=ATT_KV_HEADS * ATT_HEAD_DIM,
        n_qi=IDX_HEADS * IDX_HEAD_DIM, topk=min(IDX_TOPK_MAX, seq // 4))


def _nt_dot(a, b):
    return lax.dot_general(a, b, (((1,), (1,)), ((), ())), preferred_element_type=F32)


def _silu(v):
    return v * (1.0 / (1.0 + jnp.exp(-v)))


def _softplus(v):
    return jnp.maximum(v, 0.0) + jnp.log1p(jnp.exp(-jnp.abs(v)))


def _in_proj_kernel(x_ref, gmix_ref, w_ref, gq_ref, gk_ref, gki_ref, dtb_ref,
                    z_ref, xbc_ref, q_ref, k_ref, vt_ref, qi_ref, kia_ref, kib_ref, sm_ref, *, dm):
    x = x_ref[...]
    ms = jnp.mean(x * x, axis=-1, keepdims=True)
    xn = (x * lax.rsqrt(ms + EPS) * gmix_ref[...]).astype(BF16)

    off = [0]

    def seg(width):
        lo = off[0]
        off[0] = lo + width
        return jnp.dot(xn, w_ref[:, lo:lo + width], preferred_element_type=F32)

    z_ref[...] = seg(dm["ssd_w"])
    xbc_ref[...] = seg(dm["conv_ch"])

    def head_norm(h, gain, n_heads, out_ref):
        for i in range(n_heads):
            hh = h[:, i * ATT_HEAD_DIM:(i + 1) * ATT_HEAD_DIM]
            m2 = jnp.mean(hh * hh, axis=-1, keepdims=True)
            out_ref[:, i * ATT_HEAD_DIM:(i + 1) * ATT_HEAD_DIM] = (hh * lax.rsqrt(m2 + EPS) * gain).astype(out_ref.dtype)

    head_norm(seg(dm["n_q"]), gq_ref[...] * (ATT_HEAD_DIM ** -0.5 * LOG2_E), dm["att_heads"], q_ref)
    head_norm(seg(dm["n_kv"]), gk_ref[...], ATT_KV_HEADS, k_ref)
    vt_ref[0] = seg(dm["n_kv"]).T.astype(BF16)
    qi_ref[...] = (seg(dm["n_qi"]) * (IDX_HEAD_DIM ** -0.5)).astype(BF16)

    lane = lax.broadcasted_iota(jnp.int32, (1, LANES), 1)
    kk = seg(LANES)
    m2 = jnp.mean(kk * kk, axis=-1, keepdims=True)
    kin = kk * lax.rsqrt(m2 + EPS) * gki_ref[...]
    kia_ref[...] = jnp.where(lane < IDX_HEAD_DIM, kin, 0.0).astype(BF16)
    kib_ref[...] = jnp.where(lane >= IDX_HEAD_DIM, kin, 0.0).astype(BF16)

    sm = seg(LANES)
    wi = sm * (IDX_HEADS ** -0.5)
    dt = _softplus(sm + dtb_ref[...])
    sm_ref[...] = jnp.where(lane < DT_LO, wi, jnp.where(lane < DT_LO + dm["ssd_heads"], dt, 0.0))


def _in_proj(x2, gmix, w_all, gq, gk, gki2, dtb, *, dm, tm):
    t, d = x2.shape
    nw = w_all.shape[1]
    n_kv = dm["n_kv"]
    row = lambda width: pl.BlockSpec((tm, width), lambda i: (i, 0))
    const = lambda shape: pl.BlockSpec(shape, lambda i: (0,) * len(shape))
    out_shape = (
        jax.ShapeDtypeStruct((t, dm["ssd_w"]), F32),
        jax.ShapeDtypeStruct((t, dm["conv_ch"]), F32),
        jax.ShapeDtypeStruct((t, dm["n_q"]), BF16),
        jax.ShapeDtypeStruct((t, n_kv), BF16),
        jax.ShapeDtypeStruct((t // tm, n_kv, tm), BF16),
        jax.ShapeDtypeStruct((t, dm["n_qi"]), BF16),
        jax.ShapeDtypeStruct((t, LANES), BF16),
        jax.ShapeDtypeStruct((t, LANES), BF16),
        jax.ShapeDtypeStruct((t, LANES), F32),
    )
    out_specs = (
        row(dm["ssd_w"]), row(dm["conv_ch"]), row(dm["n_q"]), row(n_kv),
        pl.BlockSpec((1, n_kv, tm), lambda i: (i, 0, 0)),
        row(dm["n_qi"]), row(LANES), row(LANES), row(LANES),
    )
    return pl.pallas_call(
        functools.partial(_in_proj_kernel, dm=dm),
        grid=(t // tm,),
        in_specs=[
            row(d), const((1, d)),
            pl.BlockSpec((d, nw), lambda i: (0, 0), pipeline_mode=pl.Buffered(1)),
            const((1, ATT_HEAD_DIM)), const((1, ATT_HEAD_DIM)), const((1, LANES)), const((1, LANES)),
        ],
        out_specs=out_specs,
        out_shape=out_shape,
        compiler_params=pltpu.CompilerParams(dimension_semantics=("arbitrary",), vmem_limit_bytes=VMEM_LIMIT),
        name="in_proj",
    )(x2, gmix, w_all, gq, gk, gki2, dtb)


def _ssd_kernel(xbc_ref, z_ref, sm_ref, cw_ref, cb_ref, alog_ref, dskip_ref, gssd_ref, e_ref,
                y_ref, ubuf_ref, state_ref, *, dm):
    q = SSD_CHUNK
    ssd_w = dm["ssd_w"]
    n_bc = dm["n_bc"]
    gw = ssd_w // SSD_GROUPS
    hpg = dm["ssd_heads"] // SSD_GROUPS

    @pl.when(pl.program_id(1) == 0)
    def _():
        ubuf_ref[0:SUBLANES, :] = jnp.zeros((SUBLANES, ubuf_ref.shape[1]), F32)
        state_ref[...] = jnp.zeros_like(state_ref)

    u = xbc_ref[...]
    ubuf_ref[SUBLANES:SUBLANES + q, :] = u
    acc = cb_ref[...] + cw_ref[SSD_CONV - 1:SSD_CONV, :] * u
    for s in range(1, SSD_CONV):
        acc = acc + cw_ref[SSD_CONV - 1 - s:SSD_CONV - s, :] * ubuf_ref[SUBLANES - s:SUBLANES - s + q, :]
    ubuf_ref[0:SUBLANES, :] = ubuf_ref[q:q + SUBLANES, :]
    xc = _silu(acc)
    xs = xc[:, :ssd_w]
    bm = xc[:, ssd_w:ssd_w + n_bc]
    cm = xc[:, ssd_w + n_bc:ssd_w + 2 * n_bc]

    lane = lax.broadcasted_iota(jnp.int32, (1, LANES), 1)
    is_dt = (lane >= DT_LO) & (lane < DT_LO + dm["ssd_heads"])
    dt_full = jnp.where(is_dt, sm_ref[...], 0.0)
    a_full = jnp.where(is_dt, -jnp.exp(alog_ref[...]), 0.0)
    adt = dt_full * a_full

    row_i = lax.broadcasted_iota(jnp.int32, (q, q), 0)
    col_i = lax.broadcasted_iota(jnp.int32, (q, q), 1)
    causal = row_i >= col_i
    tri = causal.astype(F32)
    a_cum = jnp.dot(tri, adt, precision=HIGHEST, preferred_element_type=F32)
    a_cum_t = a_cum.T
    e = e_ref[...]
    acum_x = jnp.dot(a_cum, e, precision=HIGHEST, preferred_element_type=F32)
    dt_x = jnp.dot(dt_full, e, precision=HIGHEST, preferred_element_type=F32)
    alast_x = acum_x[q - 1:q, :]
    decay_in = jnp.exp(acum_x)
    decay_out = jnp.exp(alast_x - acum_x)
    chunk_decay = jnp.exp(alast_x)

    xdt = xs * dt_x
    xdt_b = xdt.astype(BF16)
    xw_b = (xdt * decay_out).astype(BF16)

    y_diag, y_off = [], []
    for g in range(SSD_GROUPS):
        cm_g = cm[:, g * SSD_STATE:(g + 1) * SSD_STATE]
        bm_g = bm[:, g * SSD_STATE:(g + 1) * SSD_STATE]
        cm_b = cm_g.astype(BF16)
        cb = _nt_dot(cm_b, bm_g.astype(BF16))
        s_prev = state_ref[:, g * gw:(g + 1) * gw]
        y_off.append(jnp.dot(cm_b, s_prev.astype(BF16), preferred_element_type=F32))
        state_ref[:, g * gw:(g + 1) * gw] = (
            s_prev * chunk_decay[:, g * gw:(g + 1) * gw]
            + jnp.dot(bm_g.T.astype(BF16), xw_b[:, g * gw:(g + 1) * gw], preferred_element_type=F32))
        for j in range(hpg // 2):
            h0 = g * hpg + 2 * j
            xp = xdt_b[:, h0 * SSD_HEAD_DIM:(h0 + 2) * SSD_HEAD_DIM]
            pair = None
            for r in range(2):
                c = DT_LO + h0 + r
                seg = a_cum[:, c:c + 1] - a_cum_t[c:c + 1, :]
                decay = jnp.exp(jnp.where(causal, seg, -jnp.inf))
                m = (cb * decay).astype(BF16)
                keep = (lane < SSD_HEAD_DIM) if r == 0 else (lane >= SSD_HEAD_DIM)
                part = jnp.dot(m, jnp.where(keep, xp, jnp.zeros_like(xp)), preferred_element_type=F32)
                pair = part if pair is None else pair + part
            y_diag.append(pair)

    y = jnp.concatenate(y_diag, axis=1) + decay_in * jnp.concatenate(y_off, axis=1) + dskip_ref[...] * xs
    y = y * _silu(z_ref[...])
    outs = []
    for g in range(SSD_GROUPS):
        yg = y[:, g * gw:(g + 1) * gw]
        outs.append(yg * lax.rsqrt(jnp.mean(yg * yg, axis=-1, keepdims=True) + EPS))
    y_ref[...] = (jnp.concatenate(outs, axis=1) * gssd_ref[...]).astype(y_ref.dtype)


def _ssd(xbc, z, sm, conv_w, conv_b, alog_pad, dskip_x, g_ssd, e_mat, *, dm, batch, seq):
    t = xbc.shape[0]
    nc = seq // SSD_CHUNK
    cc = dm["conv_ch"]
    ssd_w = dm["ssd_w"]
    row = lambda width: pl.BlockSpec((SSD_CHUNK, width), lambda b, c: (b * nc + c, 0))
    const = lambda shape: pl.BlockSpec(shape, lambda b, c: (0,) * len(shape))
    return pl.pallas_call(
        functools.partial(_ssd_kernel, dm=dm),
        grid=(batch, nc),
        in_specs=[row(cc), row(ssd_w), row(LANES),
                  const((SSD_CONV, cc)), const((1, cc)), const((1, LANES)), const((1, ssd_w)), const((1, ssd_w)),
                  const((LANES, ssd_w))],
        out_specs=row(ssd_w),
        out_shape=jax.ShapeDtypeStruct((t, ssd_w), BF16),
        scratch_shapes=[pltpu.VMEM((SSD_CHUNK + SUBLANES, cc), F32), pltpu.VMEM((SSD_STATE, ssd_w), F32)],
        compiler_params=pltpu.CompilerParams(dimension_semantics=("arbitrary", "arbitrary"), vmem_limit_bytes=VMEM_LIMIT),
        name="ssd",
    )(xbc, z, sm, conv_w, conv_b, alog_pad, dskip_x, g_ssd, e_mat)


def _key_to_f32(ks):
    return lax.bitcast_convert_type(ks ^ ((ks >> 31) & jnp.int32(0x7FFFFFFF)), F32)


COUNT_ACCS = 4


def _count_rows(m, cnt):
    cnt = list(cnt)
    for j, r in enumerate(range(0, m.shape[0], SUBLANES)):
        a = j % COUNT_ACCS
        cnt[a] = jnp.where(m[r:r + SUBLANES, :], cnt[a] + 1, cnt[a])
    return tuple(cnt)


def _dsa_kernel(q_ref, k_ref, vt_ref, qi_ref, kia_ref, kib_ref, sm_ref, o_ref,
                score_ref, bias_ref, thr_ref, mstar_ref, m_ref, acc_ref, logit_ref, cmax_ref, *, dm, seq, qb):
    kc = qb
    topk = dm["topk"]
    n_heads = dm["att_heads"]
    rep = n_heads // ATT_KV_HEADS
    i = pl.program_id(1)
    nkc = i + 1
    q_pos = i * qb + lax.broadcasted_iota(jnp.int32, (1, qb), 1)

    def key_pos(c):
        return c * kc + lax.broadcasted_iota(jnp.int32, (kc, 1), 0)

    def rows(c):
        return pl.ds(pl.multiple_of(c * kc, kc), kc)

    wi_t = sm_ref[...].T
    qi = qi_ref[...]

    def score_chunk(c, carry):
        ka = kia_ref[rows(c), :]
        kb = kib_ref[rows(c), :]
        acc = jnp.zeros((kc, qb), F32)
        for j in range(IDX_HEADS // 2):
            qp = qi[:, j * LANES:(j + 1) * LANES]
            w0 = wi_t[WI_LO + 2 * j:WI_LO + 2 * j + 1, :]
            w1 = wi_t[WI_LO + 2 * j + 1:WI_LO + 2 * j + 2, :]
            acc = acc + w0 * jnp.maximum(_nt_dot(ka, qp), 0.0) + w1 * jnp.maximum(_nt_dot(kb, qp), 0.0)
        score_ref[rows(c), :] = jnp.where(key_pos(c) <= q_pos, acc, -jnp.inf)
        return carry

    lax.fori_loop(0, nkc, score_chunk, 0)

    def count(pred):
        def body(c, cnt):
            return _count_rows(pred(score_ref[rows(c), :], c), cnt)
        cnt = lax.fori_loop(0, nkc, body, (jnp.zeros((SUBLANES, qb), jnp.int32),) * COUNT_ACCS)
        return jnp.sum(functools.reduce(lambda a, b: a + b, cnt), axis=0, keepdims=True)

    def bit_step(p, carry):
        res, n_res = carry
        cand = res | (jnp.int32(1) << (31 - p))
        cand_s = cand ^ jnp.int32(INT_MIN)
        cand_f = _key_to_f32(cand_s)
        n_cand = count(lambda sc, c: sc >= cand_f)
        ok = (n_cand >= topk) | (cand_s <= jnp.int32(KEY_NEG_INF))
        return jnp.where(ok, cand, res), jnp.where(ok, n_cand, n_res)

    zeros_q = jnp.zeros((1, qb), jnp.int32)
    res, n_ge = lax.fori_loop(0, 32, bit_step, (zeros_q, zeros_q))
    thr_s = res ^ jnp.int32(INT_MIN)
    thr0 = _key_to_f32(thr_s)
    thr_ref[...] = thr0
    mstar_ref[...] = jnp.full((1, qb), seq, jnp.int32)

    @pl.when(jnp.max(n_ge) > topk)
    def _():
        def refine(_, lh):
            lo, hi = lh
            mid = lo + 0.5 * (hi - lo)
            ok = count(lambda sc, c: sc >= mid) >= topk
            return jnp.where(ok, mid, lo), jnp.where(ok, hi, mid)

        thr, _ = lax.fori_loop(0, REFINE_STEPS, refine, (thr0, _key_to_f32(thr_s + 1)))
        thr_ref[...] = thr
        need = topk - count(lambda sc, c: sc > thr)
        nbits = max(1, int(np.ceil(np.log2(seq))))

        def idx_step(p, res):
            cand = res | (jnp.int32(1) << (nbits - 1 - p))
            n_eq_before = count(lambda sc, c: (sc == thr) & (key_pos(c) < cand))
            return jnp.where(n_eq_before < need, cand, res)

        mstar_ref[...] = lax.fori_loop(0, nbits, idx_step, jnp.zeros((1, qb), jnp.int32))

    thr = thr_ref[...]
    mstar = mstar_ref[...]

    def bias_chunk(c, carry):
        sc = score_ref[rows(c), :]
        kp = key_pos(c)
        sel = ((sc > thr) | ((sc == thr) & (kp <= mstar))) & (kp <= q_pos)
        bias_ref[rows(c), :] = jnp.where(sel, 0.0, NEG_BIG)
        return carry

    lax.fori_loop(0, nkc, bias_chunk, 0)

    m_ref[...] = jnp.full(m_ref.shape, NEG_BIG, F32)
    acc_ref[...] = jnp.zeros(acc_ref.shape, F32)

    ones_rows = jnp.ones((SUM_ROWS, kc), BF16)

    def logits_phase(c):
        slot = (c % 2) * n_heads
        for h in range(n_heads):
            g = h // rep
            kch = k_ref[rows(c), g * ATT_HEAD_DIM:(g + 1) * ATT_HEAD_DIM]
            logit = _nt_dot(kch, q_ref[:, h * ATT_HEAD_DIM:(h + 1) * ATT_HEAD_DIM]) + bias_ref[rows(c), :]
            logit_ref[slot + h] = logit
            cmax_ref[slot + h, 0:1, :] = jnp.max(logit, axis=0, keepdims=True)

    def softmax_phase(c):
        slot = (c % 2) * n_heads
        for h in range(n_heads):
            g = h // rep
            vt = jnp.concatenate([vt_ref[c, g * ATT_HEAD_DIM:(g + 1) * ATT_HEAD_DIM, :], ones_rows], axis=0)
            m_old = m_ref[h:h + 1, :]
            m_new = jnp.maximum(m_old, cmax_ref[slot + h, 0:1, :])
            p = jnp.exp2(logit_ref[slot + h] - m_new)
            m_ref[h:h + 1, :] = m_new
            acc_ref[h] = (acc_ref[h] * jnp.exp2(m_old - m_new)
                          + jnp.dot(vt, p.astype(BF16), preferred_element_type=F32))

    def att_step(c, carry):
        logits_phase(c)
        softmax_phase(c)
        return carry

    lax.fori_loop(0, nkc, att_step, 0)
    for h in range(n_heads):
        acc = acc_ref[h]
        o = acc[:ATT_HEAD_DIM, :] / acc[ATT_HEAD_DIM:ATT_HEAD_DIM + 1, :]
        o_ref[:, h * ATT_HEAD_DIM:(h + 1) * ATT_HEAD_DIM] = o.T.astype(o_ref.dtype)


def _dsa(q, k, vt, qi, kia, kib, sm, *, dm, batch, seq, qb):
    t = q.shape[0]
    nqb = seq // qb
    n_kv = dm["n_kv"]
    assert qb >= dm["topk"] and seq % qb == 0
    qrow = lambda width: pl.BlockSpec((qb, width), lambda b, i: (b * nqb + i, 0))
    brow = lambda width: pl.BlockSpec((seq, width), lambda b, i: (b, 0))
    return pl.pallas_call(
        functools.partial(_dsa_kernel, dm=dm, seq=seq, qb=qb),
        grid=(batch, nqb),
        in_specs=[qrow(dm["n_q"]), brow(n_kv),
                  pl.BlockSpec((nqb, n_kv, qb), lambda b, i: (b, 0, 0)),
                  qrow(dm["n_qi"]), brow(LANES), brow(LANES), qrow(LANES)],
        out_specs=qrow(dm["n_q"]),
        out_shape=jax.ShapeDtypeStruct((t, dm["n_q"]), BF16),
        scratch_shapes=[pltpu.VMEM((seq, qb), F32), pltpu.VMEM((seq, qb), F32),
                        pltpu.VMEM((1, qb), F32), pltpu.VMEM((1, qb), jnp.int32),
                        pltpu.VMEM((dm["att_heads"], qb), F32),
                        pltpu.VMEM((dm["att_heads"], ATT_HEAD_DIM + SUM_ROWS, qb), F32),
                        pltpu.VMEM((2 * dm["att_heads"], qb, qb), F32),
                        pltpu.VMEM((2 * dm["att_heads"], SUBLANES, qb), F32)],
        compiler_params=pltpu.CompilerParams(dimension_semantics=("arbitrary", "arbitrary"), vmem_limit_bytes=VMEM_LIMIT),
        name="dsa",
    )(q, k, vt, qi, kia, kib, sm)


def _out_proj_kernel(x_ref, ya_ref, yb_ref, w_ref, o_ref):
    na = ya_ref.shape[1]
    o_ref[...] = (x_ref[...]
                  + jnp.dot(ya_ref[...], w_ref[:na, :], preferred_element_type=F32)
                  + jnp.dot(yb_ref[...], w_ref[na:, :], preferred_element_type=F32))


def _out_proj(x2, y_ssd, y_att, w_out, *, tm):
    t, d = x2.shape
    row = lambda width: pl.BlockSpec((tm, width), lambda i: (i, 0))
    return pl.pallas_call(
        _out_proj_kernel,
        grid=(t // tm,),
        in_specs=[row(d), row(y_ssd.shape[1]), row(y_att.shape[1]),
                  pl.BlockSpec(w_out.shape, lambda i: (0, 0), pipeline_mode=pl.Buffered(1))],
        out_specs=row(d),
        out_shape=jax.ShapeDtypeStruct((t, d), F32),
        compiler_params=pltpu.CompilerParams(dimension_semantics=("arbitrary",), vmem_limit_bytes=VMEM_LIMIT),
        name="out_proj",
    )(x2, y_ssd, y_att, w_out)


def _mlp_kernel(x_ref, g_ref, wu_ref, wd_ref, o_ref, xn_ref, acc_ref):
    f = pl.program_id(1)

    @pl.when(f == 0)
    def _():
        x = x_ref[...]
        ms = jnp.mean(x * x, axis=-1, keepdims=True)
        xn_ref[...] = (x * lax.rsqrt(ms + EPS) * g_ref[...]).astype(BF16)
        acc_ref[...] = jnp.zeros_like(acc_ref)

    u = jnp.maximum(jnp.dot(xn_ref[...], wu_ref[...], preferred_element_type=F32), 0.0)
    acc_ref[...] += jnp.dot((u * u).astype(BF16), wd_ref[...], preferred_element_type=F32)

    @pl.when(f == pl.num_programs(1) - 1)
    def _():
        o_ref[...] = x_ref[...] + acc_ref[...]


def _mlp(x2, g_mlp, w_up, w_down, *, tm, tf):
    t, d = x2.shape
    d_ff = w_up.shape[1]
    return pl.pallas_call(
        _mlp_kernel,
        grid=(t // tm, d_ff // tf),
        in_specs=[pl.BlockSpec((tm, d), lambda i, f: (i, 0)),
                  pl.BlockSpec((1, d), lambda i, f: (0, 0)),
                  pl.BlockSpec((d, tf), lambda i, f: (0, f)),
                  pl.BlockSpec((tf, d), lambda i, f: (f, 0))],
        out_specs=pl.BlockSpec((tm, d), lambda i, f: (i, 0)),
        out_shape=jax.ShapeDtypeStruct((t, d), F32),
        scratch_shapes=[pltpu.VMEM((tm, d), BF16), pltpu.VMEM((tm, d), F32)],
        compiler_params=pltpu.CompilerParams(dimension_semantics=("arbitrary", "arbitrary"), vmem_limit_bytes=VMEM_LIMIT),
        name="mlp",
    )(x2, g_mlp, w_up, w_down)


def _pack_w_in(w_in, dm):
    sizes = (dm["ssd_w"], dm["conv_ch"], dm["ssd_heads"], dm["n_q"], dm["n_kv"], dm["n_kv"], dm["n_qi"],
             IDX_HEAD_DIM, IDX_HEADS)
    pts = [int(v) for v in np.cumsum(sizes)[:-1]]
    z, xbc, dt, q, k, v, qi, ki, wi = jnp.split(w_in, pts, axis=1)
    pad = jnp.zeros((w_in.shape[0], LANES - IDX_HEADS - dm["ssd_heads"]), w_in.dtype)
    return jnp.concatenate([z, xbc, q, k, v, qi, ki, ki, wi, dt, pad], axis=1)


def _expand_matrix(dm):
    e = np.zeros((LANES, dm["ssd_w"]), np.float32)
    for h in range(dm["ssd_heads"]):
        e[DT_LO + h, h * SSD_HEAD_DIM:(h + 1) * SSD_HEAD_DIM] = 1.0
    return jnp.asarray(e)


def _lane_pad(v, lo):
    return jnp.zeros((1, LANES), F32).at[0, lo:lo + v.shape[0]].set(v)


def kernel(x, g_mix, w_in, conv_w, conv_b, dt_bias, a_log, d_skip, g_ssd, g_q, g_k, g_kidx, w_out, g_mlp, w_up, w_down):
    batch, seq, d = x.shape
    depth = w_in.shape[0]
    dm = _dims(d, seq)
    assert dm["ssd_heads"] + IDX_HEADS <= LANES and (dm["ssd_heads"] // SSD_GROUPS) % 2 == 0
    assert seq % SSD_CHUNK == 0 and dm["att_heads"] % ATT_KV_HEADS == 0
    qb = 256
    tm = min(512, batch * seq)
    tf = min(1024, w_up.shape[2])
    e_mat = _expand_matrix(dm)
    w_in, w_out, w_up, w_down = (w.astype(BF16) for w in (w_in, w_out, w_up, w_down))

    x2 = x.reshape(batch * seq, d)
    for i in range(depth):
        w_all = _pack_w_in(w_in[i], dm)
        z, xbc, q, k, vt, qi, kia, kib, sm = _in_proj(
            x2, g_mix[i][None, :], w_all, g_q[i][None, :], g_k[i][None, :],
            jnp.concatenate([g_kidx[i], g_kidx[i]])[None, :], _lane_pad(dt_bias[i], DT_LO), dm=dm, tm=qb)
        y_ssd = _ssd(xbc, z, sm, conv_w[i], conv_b[i][None, :], _lane_pad(a_log[i], DT_LO),
                     jnp.repeat(d_skip[i], SSD_HEAD_DIM)[None, :], g_ssd[i][None, :], e_mat,
                     dm=dm, batch=batch, seq=seq)
        y_att = _dsa(q, k, vt, qi, kia, kib, sm, dm=dm, batch=batch, seq=seq, qb=qb)
        x2 = _out_proj(x2, y_ssd, y_att, w_out[i], tm=tm)
        x2 = _mlp(x2, g_mlp[i][None, :], w_up[i], w_down[i], tm=tm, tf=tf)
    return x2.reshape(batch, seq, d)
```

```python
import functools

import numpy as np
import jax
import jax.numpy as jnp
from jax import lax
from jax.experimental import pallas as pl
from jax.experimental.pallas import tpu as pltpu

EPS = 1e-6
SSD_HEAD_DIM = 64
SSD_GROUPS = 2
SSD_STATE = 128
SSD_CONV = 4
SSD_CHUNK = 128
ATT_HEAD_DIM = 128
ATT_KV_HEADS = 4
IDX_HEADS = 16
IDX_HEAD_DIM = 64
IDX_TOPK_MAX = 256

LANES = 128
SUBLANES = 8
VMEM_LIMIT = 56 * 1024 * 1024

WI_LO, DT_LO = 0, 16
NEG_BIG = -1e30
INT_MIN = -(2 ** 31)
KEY_NEG_INF = 0x807FFFFF - 2 ** 32
REFINE_STEPS = 30
LOG2_E = 1.4426950408889634
SUM_ROWS = 16

F32 = jnp.float32
BF16 = jnp.bfloat16
HIGHEST = lax.Precision.HIGHEST


def _dims(d_model, seq):
    ssd_w = d_model // 2
    att_w = d_model - ssd_w
    n_bc = SSD_GROUPS * SSD_STATE
    return dict(
        ssd_w=ssd_w, ssd_heads=ssd_w // SSD_HEAD_DIM, n_bc=n_bc, conv_ch=ssd_w + 2 * n_bc,
        n_q=att_w, att_heads=att_w // ATT_HEAD_DIM, n_kv=ATT_KV_HEADS * ATT_HEAD_DIM,
        n_qi=IDX_HEADS * IDX_HEAD_DIM, topk=min(IDX_TOPK_MAX, seq // 4))


def _nt_dot(a, b):
    return lax.dot_general(a, b, (((1,), (1,)), ((), ())), preferred_element_type=F32)


def _silu(v):
    return v * (1.0 / (1.0 + jnp.exp(-v)))


def _softplus(v):
    return jnp.maximum(v, 0.0) + jnp.log1p(jnp.exp(-jnp.abs(v)))


def _split3(a):
    hi = a.astype(BF16)
    rest = a - hi.astype(F32)
    mid = rest.astype(BF16)
    lo = (rest - mid.astype(F32)).astype(BF16)
    return jnp.concatenate([hi, mid, lo], axis=1)


def _in_proj_kernel(x_ref, gmix_ref, w_ref, gq_ref, gk_ref, gki_ref, dtb_ref,
                    z_ref, xbc_ref, q_ref, k_ref, vt_ref, qi_ref, kia_ref, kib_ref, sm_ref, *, dm):
    x = x_ref[...]
    ms = jnp.mean(x * x, axis=-1, keepdims=True)
    xn = (x * lax.rsqrt(ms + EPS) * gmix_ref[...]).astype(BF16)

    off = [0]

    def seg(width):
        lo = off[0]
        off[0] = lo + width
        return jnp.dot(xn, w_ref[:, lo:lo + width], preferred_element_type=F32)

    z_ref[...] = seg(dm["ssd_w"])
    xbc_ref[...] = seg(dm["conv_ch"])

    def head_norm(h, gain, n_heads, out_ref):
        for i in range(n_heads):
            hh = h[:, i * ATT_HEAD_DIM:(i + 1) * ATT_HEAD_DIM]
            m2 = jnp.mean(hh * hh, axis=-1, keepdims=True)
            out_ref[:, i * ATT_HEAD_DIM:(i + 1) * ATT_HEAD_DIM] = (hh * lax.rsqrt(m2 + EPS) * gain).astype(out_ref.dtype)

    head_norm(seg(dm["n_q"]), gq_ref[...] * (ATT_HEAD_DIM ** -0.5 * LOG2_E), dm["att_heads"], q_ref)
    head_norm(seg(dm["n_kv"]), gk_ref[...], ATT_KV_HEADS, k_ref)
    vt_ref[0] = seg(dm["n_kv"]).T.astype(BF16)
    qi_ref[...] = (seg(dm["n_qi"]) * (IDX_HEAD_DIM ** -0.5)).astype(BF16)

    lane = lax.broadcasted_iota(jnp.int32, (1, LANES), 1)
    kk = seg(LANES)
    m2 = jnp.mean(kk * kk, axis=-1, keepdims=True)
    kin = kk * lax.rsqrt(m2 + EPS) * gki_ref[...]
    kia_ref[...] = jnp.where(lane < IDX_HEAD_DIM, kin, 0.0).astype(BF16)
    kib_ref[...] = jnp.where(lane >= IDX_HEAD_DIM, kin, 0.0).astype(BF16)

    sm = seg(LANES)
    wi = sm * (IDX_HEADS ** -0.5)
    dt = _softplus(sm + dtb_ref[...])
    sm_ref[...] = jnp.where(lane < DT_LO, wi, jnp.where(lane < DT_LO + dm["ssd_heads"], dt, 0.0))


def _in_proj(x2, gmix, w_all, gq, gk, gki2, dtb, *, layer, dm, tm):
    t, d = x2.shape
    nw = w_all.shape[2]
    n_kv = dm["n_kv"]
    row = lambda width: pl.BlockSpec((tm, width), lambda i: (i, 0))
    const = lambda shape: pl.BlockSpec(shape, lambda i: (0,) * len(shape))
    out_shape = (
        jax.ShapeDtypeStruct((t, dm["ssd_w"]), F32),
        jax.ShapeDtypeStruct((t, dm["conv_ch"]), F32),
        jax.ShapeDtypeStruct((t, dm["n_q"]), BF16),
        jax.ShapeDtypeStruct((t, n_kv), BF16),
        jax.ShapeDtypeStruct((t // tm, n_kv, tm), BF16),
        jax.ShapeDtypeStruct((t, dm["n_qi"]), BF16),
        jax.ShapeDtypeStruct((t, LANES), BF16),
        jax.ShapeDtypeStruct((t, LANES), BF16),
        jax.ShapeDtypeStruct((t, LANES), F32),
    )
    out_specs = (
        row(dm["ssd_w"]), row(dm["conv_ch"]), row(dm["n_q"]), row(n_kv),
        pl.BlockSpec((1, n_kv, tm), lambda i: (i, 0, 0)),
        row(dm["n_qi"]), row(LANES), row(LANES), row(LANES),
    )
    return pl.pallas_call(
        functools.partial(_in_proj_kernel, dm=dm),
        grid=(t // tm,),
        in_specs=[
            row(d), const((1, d)),
            pl.BlockSpec((None, d, nw), lambda i: (layer, 0, 0), pipeline_mode=pl.Buffered(1)),
            const((1, ATT_HEAD_DIM)), const((1, ATT_HEAD_DIM)), const((1, LANES)), const((1, LANES)),
        ],
        out_specs=out_specs,
        out_shape=out_shape,
        compiler_params=pltpu.CompilerParams(dimension_semantics=("arbitrary",), vmem_limit_bytes=VMEM_LIMIT),
        name="in_proj",
    )(x2, gmix, w_all, gq, gk, gki2, dtb)


def _ssd_kernel(xbc_ref, z_ref, sm_ref, cw_ref, cb_ref, alog_ref, dskip_ref, gssd_ref, e_ref,
                y_ref, ubuf_ref, state_ref, *, dm):
    q = SSD_CHUNK
    ssd_w = dm["ssd_w"]
    n_bc = dm["n_bc"]
    gw = ssd_w // SSD_GROUPS
    hpg = dm["ssd_heads"] // SSD_GROUPS

    @pl.when(pl.program_id(1) == 0)
    def _():
        ubuf_ref[0:SUBLANES, :] = jnp.zeros((SUBLANES, ubuf_ref.shape[1]), F32)
        state_ref[...] = jnp.zeros_like(state_ref)

    u = xbc_ref[...]
    ubuf_ref[SUBLANES:SUBLANES + q, :] = u
    acc = cb_ref[...] + cw_ref[SSD_CONV - 1:SSD_CONV, :] * u
    for s in range(1, SSD_CONV):
        acc = acc + cw_ref[SSD_CONV - 1 - s:SSD_CONV - s, :] * ubuf_ref[SUBLANES - s:SUBLANES - s + q, :]
    ubuf_ref[0:SUBLANES, :] = ubuf_ref[q:q + SUBLANES, :]
    xc = _silu(acc)
    xs = xc[:, :ssd_w]
    bm = xc[:, ssd_w:ssd_w + n_bc]
    cm = xc[:, ssd_w + n_bc:ssd_w + 2 * n_bc]

    lane = lax.broadcasted_iota(jnp.int32, (1, LANES), 1)
    is_dt = (lane >= DT_LO) & (lane < DT_LO + dm["ssd_heads"])
    dt_full = jnp.where(is_dt, sm_ref[...], 0.0)
    a_full = jnp.where(is_dt, -jnp.exp(alog_ref[...]), 0.0)
    adt = dt_full * a_full

    row_i = lax.broadcasted_iota(jnp.int32, (q, q), 0)
    col_i = lax.broadcasted_iota(jnp.int32, (q, q), 1)
    causal = row_i >= col_i
    tri = causal.astype(F32)
    a_cum = jnp.dot(tri, adt, precision=HIGHEST, preferred_element_type=F32)
    a_cum_t = a_cum.T
    both = jnp.dot(_split3(jnp.concatenate([a_cum, dt_full], axis=0)), e_ref[...], preferred_element_type=F32)
    acum_x = both[:q]
    dt_x = both[q:]
    alast_x = acum_x[q - 1:q, :]
    decay_in = jnp.exp(acum_x)
    decay_out = jnp.exp(alast_x - acum_x)
    chunk_decay = jnp.exp(alast_x)

    xdt = xs * dt_x
    xdt_b = xdt.astype(BF16)
    xw_b = (xdt * decay_out).astype(BF16)

    y_diag, y_off = [], []
    for g in range(SSD_GROUPS):
        cm_g = cm[:, g * SSD_STATE:(g + 1) * SSD_STATE]
        bm_g = bm[:, g * SSD_STATE:(g + 1) * SSD_STATE]
        cm_b = cm_g.astype(BF16)
        cb = _nt_dot(cm_b, bm_g.astype(BF16))
        s_prev = state_ref[:, g * gw:(g + 1) * gw]
        y_off.append(jnp.dot(cm_b, s_prev.astype(BF16), preferred_element_type=F32))
        state_ref[:, g * gw:(g + 1) * gw] = (
            s_prev * chunk_decay[:, g * gw:(g + 1) * gw]
            + jnp.dot(bm_g.T.astype(BF16), xw_b[:, g * gw:(g + 1) * gw], preferred_element_type=F32))
        for j in range(hpg // 2):
            h0 = g * hpg + 2 * j
            xp = xdt_b[:, h0 * SSD_HEAD_DIM:(h0 + 2) * SSD_HEAD_DIM]
            pair = None
            for r in range(2):
                c = DT_LO + h0 + r
                seg = a_cum[:, c:c + 1] - a_cum_t[c:c + 1, :]
                decay = jnp.exp(jnp.where(causal, seg, -jnp.inf))
                m = (cb * decay).astype(BF16)
                keep = (lane < SSD_HEAD_DIM) if r == 0 else (lane >= SSD_HEAD_DIM)
                part = jnp.dot(m, jnp.where(keep, xp, jnp.zeros_like(xp)), preferred_element_type=F32)
                pair = part if pair is None else pair + part
            y_diag.append(pair)

    y = jnp.concatenate(y_diag, axis=1) + decay_in * jnp.concatenate(y_off, axis=1) + dskip_ref[...] * xs
    y = y * _silu(z_ref[...])
    outs = []
    for g in range(SSD_GROUPS):
        yg = y[:, g * gw:(g + 1) * gw]
        outs.append(yg * lax.rsqrt(jnp.mean(yg * yg, axis=-1, keepdims=True) + EPS))
    y_ref[...] = (jnp.concatenate(outs, axis=1) * gssd_ref[...]).astype(y_ref.dtype)


def _ssd(xbc, z, sm, conv_w, conv_b, alog_pad, dskip_x, g_ssd, e_mat, *, dm, batch, seq):
    t = xbc.shape[0]
    nc = seq // SSD_CHUNK
    cc = dm["conv_ch"]
    ssd_w = dm["ssd_w"]
    row = lambda width: pl.BlockSpec((SSD_CHUNK, width), lambda b, c: (b * nc + c, 0))
    const = lambda shape: pl.BlockSpec(shape, lambda b, c: (0,) * len(shape))
    return pl.pallas_call(
        functools.partial(_ssd_kernel, dm=dm),
        grid=(batch, nc),
        in_specs=[row(cc), row(ssd_w), row(LANES),
                  const((SSD_CONV, cc)), const((1, cc)), const((1, LANES)), const((1, ssd_w)), const((1, ssd_w)),
                  const((3 * LANES, ssd_w))],
        out_specs=row(ssd_w),
        out_shape=jax.ShapeDtypeStruct((t, ssd_w), BF16),
        scratch_shapes=[pltpu.VMEM((SSD_CHUNK + SUBLANES, cc), F32), pltpu.VMEM((SSD_STATE, ssd_w), F32)],
        compiler_params=pltpu.CompilerParams(dimension_semantics=("arbitrary", "arbitrary"), vmem_limit_bytes=VMEM_LIMIT),
        name="ssd",
    )(xbc, z, sm, conv_w, conv_b, alog_pad, dskip_x, g_ssd, e_mat)


def _key_to_f32(ks):
    return lax.bitcast_convert_type(ks ^ ((ks >> 31) & jnp.int32(0x7FFFFFFF)), F32)


COUNT_ACCS = 4


def _count_rows(m, cnt):
    cnt = list(cnt)
    for j, r in enumerate(range(0, m.shape[0], SUBLANES)):
        a = j % COUNT_ACCS
        cnt[a] = jnp.where(m[r:r + SUBLANES, :], cnt[a] + 1, cnt[a])
    return tuple(cnt)


def _dsa_kernel(q_ref, k_ref, vt_ref, qi_ref, kia_ref, kib_ref, sm_ref, o_ref,
                score_ref, bias_ref, thr_ref, mstar_ref, m_ref, acc_ref, logit_ref, cmax_ref, *, dm, seq, qb):
    kc = qb
    topk = dm["topk"]
    n_heads = dm["att_heads"]
    rep = n_heads // ATT_KV_HEADS
    i = pl.program_id(1)
    nkc = i + 1
    q_pos = i * qb + lax.broadcasted_iota(jnp.int32, (1, qb), 1)

    def key_pos(c):
        return c * kc + lax.broadcasted_iota(jnp.int32, (kc, 1), 0)

    def rows(c):
        return pl.ds(pl.multiple_of(c * kc, kc), kc)

    wi_t = sm_ref[...].T
    qi = qi_ref[...]

    def score_chunk(c):
        ka = kia_ref[rows(c), :]
        kb = kib_ref[rows(c), :]
        acc = jnp.zeros((kc, qb), F32)
        for j in range(IDX_HEADS // 2):
            qp = qi[:, j * LANES:(j + 1) * LANES]
            w0 = wi_t[WI_LO + 2 * j:WI_LO + 2 * j + 1, :]
            w1 = wi_t[WI_LO + 2 * j + 1:WI_LO + 2 * j + 2, :]
            acc = acc + w0 * jnp.maximum(_nt_dot(ka, qp), 0.0) + w1 * jnp.maximum(_nt_dot(kb, qp), 0.0)
        score_ref[rows(c), :] = jnp.where(key_pos(c) <= q_pos, acc, -jnp.inf)

    def pairwise(one_chunk):
        def two(j, carry):
            one_chunk(2 * j)
            one_chunk(2 * j + 1)
            return carry

        lax.fori_loop(0, nkc // 2, two, 0)

        @pl.when(nkc % 2 == 1)
        def _():
            one_chunk(nkc - 1)

    pairwise(score_chunk)

    def count(pred):
        def body(c, cnt):
            return _count_rows(pred(score_ref[rows(c), :], c), cnt)
        cnt = lax.fori_loop(0, nkc, body, (jnp.zeros((SUBLANES, qb), jnp.int32),) * COUNT_ACCS)
        return jnp.sum(functools.reduce(lambda a, b: a + b, cnt), axis=0, keepdims=True)

    def bit_step(p, carry):
        res, n_res = carry
        cand = res | (jnp.int32(1) << (31 - p))
        cand_s = cand ^ jnp.int32(INT_MIN)
        cand_f = _key_to_f32(cand_s)
        n_cand = count(lambda sc, c: sc >= cand_f)
        ok = (n_cand >= topk) | (cand_s <= jnp.int32(KEY_NEG_INF))
        return jnp.where(ok, cand, res), jnp.where(ok, n_cand, n_res)

    zeros_q = jnp.zeros((1, qb), jnp.int32)
    res, n_ge = lax.fori_loop(0, 32, bit_step, (zeros_q, zeros_q))
    thr_s = res ^ jnp.int32(INT_MIN)
    thr0 = _key_to_f32(thr_s)
    thr_ref[...] = thr0
    mstar_ref[...] = jnp.full((1, qb), seq, jnp.int32)

    @pl.when(jnp.max(n_ge) > topk)
    def _():
        def refine(_, lh):
            lo, hi = lh
            mid = lo + 0.5 * (hi - lo)
            ok = count(lambda sc, c: sc >= mid) >= topk
            return jnp.where(ok, mid, lo), jnp.where(ok, hi, mid)

        thr, _ = lax.fori_loop(0, REFINE_STEPS, refine, (thr0, _key_to_f32(thr_s + 1)))
        thr_ref[...] = thr
        need = topk - count(lambda sc, c: sc > thr)
        nbits = max(1, int(np.ceil(np.log2(seq))))

        def idx_step(p, res):
            cand = res | (jnp.int32(1) << (nbits - 1 - p))
            n_eq_before = count(lambda sc, c: (sc == thr) & (key_pos(c) < cand))
            return jnp.where(n_eq_before < need, cand, res)

        mstar_ref[...] = lax.fori_loop(0, nbits, idx_step, jnp.zeros((1, qb), jnp.int32))

    thr = thr_ref[...]
    mstar = mstar_ref[...]

    def bias_chunk(c, carry):
        sc = score_ref[rows(c), :]
        kp = key_pos(c)
        sel = ((sc > thr) | ((sc == thr) & (kp <= mstar))) & (kp <= q_pos)
        bias_ref[rows(c), :] = jnp.where(sel, 0.0, NEG_BIG)
        return carry

    lax.fori_loop(0, nkc, bias_chunk, 0)

    m_ref[...] = jnp.full(m_ref.shape, NEG_BIG, F32)
    acc_ref[...] = jnp.zeros(acc_ref.shape, F32)

    ones_rows = jnp.ones((SUM_ROWS, kc), BF16)

    def logits_phase(c):
        slot = (c % 2) * n_heads
        for h in range(n_heads):
            g = h // rep
            kch = k_ref[rows(c), g * ATT_HEAD_DIM:(g + 1) * ATT_HEAD_DIM]
            logit = _nt_dot(kch, q_ref[:, h * ATT_HEAD_DIM:(h + 1) * ATT_HEAD_DIM]) + bias_ref[rows(c), :]
            logit_ref[slot + h] = logit
            cmax_ref[slot + h, 0:1, :] = jnp.max(logit, axis=0, keepdims=True)

    def softmax_phase(c):
        slot = (c % 2) * n_heads
        for h in range(n_heads):
            g = h // rep
            vt = jnp.concatenate([vt_ref[c, g * ATT_HEAD_DIM:(g + 1) * ATT_HEAD_DIM, :], ones_rows], axis=0)
            m_old = m_ref[h:h + 1, :]
            m_new = jnp.maximum(m_old, cmax_ref[slot + h, 0:1, :])
            p = jnp.exp2(logit_ref[slot + h] - m_new)
            m_ref[h:h + 1, :] = m_new
            acc_ref[h] = (acc_ref[h] * jnp.exp2(m_old - m_new)
                          + jnp.dot(vt, p.astype(BF16), preferred_element_type=F32))

    def att_chunk(c):
        logits_phase(c)
        softmax_phase(c)

    pairwise(att_chunk)
    for h in range(n_heads):
        acc = acc_ref[h]
        o = acc[:ATT_HEAD_DIM, :] / acc[ATT_HEAD_DIM:ATT_HEAD_DIM + 1, :]
        o_ref[:, h * ATT_HEAD_DIM:(h + 1) * ATT_HEAD_DIM] = o.T.astype(o_ref.dtype)


def _dsa(q, k, vt, qi, kia, kib, sm, *, dm, batch, seq, qb):
    t = q.shape[0]
    nqb = seq // qb
    n_kv = dm["n_kv"]
    assert qb >= dm["topk"] and seq % qb == 0
    qrow = lambda width: pl.BlockSpec((qb, width), lambda b, i: (b * nqb + i, 0))
    brow = lambda width: pl.BlockSpec((seq, width), lambda b, i: (b, 0))
    return pl.pallas_call(
        functools.partial(_dsa_kernel, dm=dm, seq=seq, qb=qb),
        grid=(batch, nqb),
        in_specs=[qrow(dm["n_q"]), brow(n_kv),
                  pl.BlockSpec((nqb, n_kv, qb), lambda b, i: (b, 0, 0)),
                  qrow(dm["n_qi"]), brow(LANES), brow(LANES), qrow(LANES)],
        out_specs=qrow(dm["n_q"]),
        out_shape=jax.ShapeDtypeStruct((t, dm["n_q"]), BF16),
        scratch_shapes=[pltpu.VMEM((seq, qb), F32), pltpu.VMEM((seq, qb), F32),
                        pltpu.VMEM((1, qb), F32), pltpu.VMEM((1, qb), jnp.int32),
                        pltpu.VMEM((dm["att_heads"], qb), F32),
                        pltpu.VMEM((dm["att_heads"], ATT_HEAD_DIM + SUM_ROWS, qb), F32),
                        pltpu.VMEM((2 * dm["att_heads"], qb, qb), F32),
                        pltpu.VMEM((2 * dm["att_heads"], SUBLANES, qb), F32)],
        compiler_params=pltpu.CompilerParams(dimension_semantics=("arbitrary", "arbitrary"), vmem_limit_bytes=VMEM_LIMIT),
        name="dsa",
    )(q, k, vt, qi, kia, kib, sm)


def _out_proj_kernel(x_ref, ya_ref, yb_ref, w_ref, o_ref):
    na = ya_ref.shape[1]
    o_ref[...] = (x_ref[...]
                  + jnp.dot(ya_ref[...], w_ref[:na, :], preferred_element_type=F32)
                  + jnp.dot(yb_ref[...], w_ref[na:, :], preferred_element_type=F32))


def _out_proj(x2, y_ssd, y_att, w_out, *, layer, tm):
    t, d = x2.shape
    row = lambda width: pl.BlockSpec((tm, width), lambda i: (i, 0))
    return pl.pallas_call(
        _out_proj_kernel,
        grid=(t // tm,),
        in_specs=[row(d), row(y_ssd.shape[1]), row(y_att.shape[1]),
                  pl.BlockSpec((None,) + w_out.shape[1:], lambda i: (layer, 0, 0), pipeline_mode=pl.Buffered(1))],
        out_specs=row(d),
        out_shape=jax.ShapeDtypeStruct((t, d), F32),
        compiler_params=pltpu.CompilerParams(dimension_semantics=("arbitrary",), vmem_limit_bytes=VMEM_LIMIT),
        name="out_proj",
    )(x2, y_ssd, y_att, w_out)


def _mlp_kernel(x_ref, g_ref, wu_ref, wd_ref, o_ref, xn_ref, acc_ref):
    f = pl.program_id(1)

    @pl.when(f == 0)
    def _():
        x = x_ref[...]
        ms = jnp.mean(x * x, axis=-1, keepdims=True)
        xn_ref[...] = (x * lax.rsqrt(ms + EPS) * g_ref[...]).astype(BF16)
        acc_ref[...] = jnp.zeros_like(acc_ref)

    u = jnp.maximum(jnp.dot(xn_ref[...], wu_ref[...], preferred_element_type=F32), 0.0)
    acc_ref[...] += jnp.dot((u * u).astype(BF16), wd_ref[...], preferred_element_type=F32)

    @pl.when(f == pl.num_programs(1) - 1)
    def _():
        o_ref[...] = x_ref[...] + acc_ref[...]


def _mlp(x2, g_mlp, w_up, w_down, *, layer, tm, tf):
    t, d = x2.shape
    d_ff = w_up.shape[2]
    return pl.pallas_call(
        _mlp_kernel,
        grid=(t // tm, d_ff // tf),
        in_specs=[pl.BlockSpec((tm, d), lambda i, f: (i, 0)),
                  pl.BlockSpec((1, d), lambda i, f: (0, 0)),
                  pl.BlockSpec((None, d, tf), lambda i, f: (layer, 0, f)),
                  pl.BlockSpec((None, tf, d), lambda i, f: (layer, f, 0))],
        out_specs=pl.BlockSpec((tm, d), lambda i, f: (i, 0)),
        out_shape=jax.ShapeDtypeStruct((t, d), F32),
        scratch_shapes=[pltpu.VMEM((tm, d), BF16), pltpu.VMEM((tm, d), F32)],
        compiler_params=pltpu.CompilerParams(dimension_semantics=("arbitrary", "arbitrary"), vmem_limit_bytes=VMEM_LIMIT),
        name="mlp",
    )(x2, g_mlp, w_up, w_down)


def _pack_w_in(w_in, dm):
    sizes = (dm["ssd_w"], dm["conv_ch"], dm["ssd_heads"], dm["n_q"], dm["n_kv"], dm["n_kv"], dm["n_qi"],
             IDX_HEAD_DIM, IDX_HEADS)
    pts = [int(v) for v in np.cumsum(sizes)[:-1]]
    z, xbc, dt, q, k, v, qi, ki, wi = jnp.split(w_in, pts, axis=2)
    pad = jnp.zeros(w_in.shape[:2] + (LANES - IDX_HEADS - dm["ssd_heads"],), w_in.dtype)
    return jnp.concatenate([z, xbc, q, k, v, qi, ki, ki, wi, dt, pad], axis=2).astype(BF16)


def _expand_matrix(dm):
    e = np.zeros((LANES, dm["ssd_w"]), np.float32)
    for h in range(dm["ssd_heads"]):
        e[DT_LO + h, h * SSD_HEAD_DIM:(h + 1) * SSD_HEAD_DIM] = 1.0
    return jnp.asarray(np.concatenate([e, e, e], axis=0)).astype(BF16)


def _lane_pad(v, lo):
    return jnp.zeros((1, LANES), F32).at[0, lo:lo + v.shape[0]].set(v)


def kernel(x, g_mix, w_in, conv_w, conv_b, dt_bias, a_log, d_skip, g_ssd, g_q, g_k, g_kidx, w_out, g_mlp, w_up, w_down):
    batch, seq, d = x.shape
    depth = w_in.shape[0]
    dm = _dims(d, seq)
    assert dm["ssd_heads"] + IDX_HEADS <= LANES and (dm["ssd_heads"] // SSD_GROUPS) % 2 == 0
    assert seq % SSD_CHUNK == 0 and dm["att_heads"] % ATT_KV_HEADS == 0
    qb = 256
    tm = min(512, batch * seq)
    tf = min(1024, w_up.shape[2])
    e_mat = _expand_matrix(dm)
    w_all = _pack_w_in(w_in, dm)
    w_out, w_up, w_down = (w.astype(BF16) for w in (w_out, w_up, w_down))

    x2 = x.reshape(batch * seq, d)
    for i in range(depth):
        z, xbc, q, k, vt, qi, kia, kib, sm = _in_proj(
            x2, g_mix[i][None, :], w_all, g_q[i][None, :], g_k[i][None, :],
            jnp.concatenate([g_kidx[i], g_kidx[i]])[None, :], _lane_pad(dt_bias[i], DT_LO), layer=i, dm=dm, tm=qb)
        y_ssd = _ssd(xbc, z, sm, conv_w[i], conv_b[i][None, :], _lane_pad(a_log[i], DT_LO),
                     jnp.repeat(d_skip[i], SSD_HEAD_DIM)[None, :], g_ssd[i][None, :], e_mat,
                     dm=dm, batch=batch, seq=seq)
        y_att = _dsa(q, k, vt, qi, kia, kib, sm, dm=dm, batch=batch, seq=seq, qb=qb)
        x2 = _out_proj(x2, y_ssd, y_att, w_out, layer=i, tm=tm)
        x2 = _mlp(x2, g_mlp[i][None, :], w_up, w_down, layer=i, tm=tm, tf=tf)
    return x2.reshape(batch, seq, d)
```

```python
import functools

import numpy as np
import jax
import jax.numpy as jnp
from jax import lax
from jax.experimental import pallas as pl
from jax.experimental.pallas import tpu as pltpu

EPS = 1e-6
SSD_HEAD_DIM = 64
SSD_GROUPS = 2
SSD_STATE = 128
SSD_CONV = 4
SSD_CHUNK = 128
ATT_HEAD_DIM = 128
ATT_KV_HEADS = 4
IDX_HEADS = 16
IDX_HEAD_DIM = 64
IDX_TOPK_MAX = 256

LANES = 128
SUBLANES = 8
VMEM_LIMIT = 56 * 1024 * 1024

WI_LO, DT_LO = 0, 16
NEG_BIG = -1e30
INT_MIN = -(2 ** 31)
KEY_NEG_INF = 0x807FFFFF - 2 ** 32
REFINE_STEPS = 30
LOG2_E = 1.4426950408889634
SUM_ROWS = 16
SSD_STEP_CHUNKS = 4

F32 = jnp.float32
BF16 = jnp.bfloat16
HIGHEST = lax.Precision.HIGHEST


def _dims(d_model, seq):
    ssd_w = d_model // 2
    att_w = d_model - ssd_w
    n_bc = SSD_GROUPS * SSD_STATE
    return dict(
        ssd_w=ssd_w, ssd_heads=ssd_w // SSD_HEAD_DIM, n_bc=n_bc, conv_ch=ssd_w + 2 * n_bc,
        n_q=att_w, att_heads=att_w // ATT_HEAD_DIM, n_kv=ATT_KV_HEADS * ATT_HEAD_DIM,
        n_qi=IDX_HEADS * IDX_HEAD_DIM, topk=min(IDX_TOPK_MAX, seq // 4))


def _nt_dot(a, b):
    return lax.dot_general(a, b, (((1,), (1,)), ((), ())), preferred_element_type=F32)


def _silu(v):
    return v * (1.0 / (1.0 + jnp.exp(-v)))


def _softplus(v):
    return jnp.maximum(v, 0.0) + jnp.log1p(jnp.exp(-jnp.abs(v)))


def _split3(a):
    hi = a.astype(BF16)
    rest = a - hi.astype(F32)
    mid = rest.astype(BF16)
    lo = (rest - mid.astype(F32)).astype(BF16)
    return jnp.concatenate([hi, mid, lo], axis=1)


def _in_proj_kernel(x_ref, gmix_ref, w_ref, gq_ref, gk_ref, gki_ref, dtb_ref, cw_ref, cb_ref,
                    gz_ref, xc_ref, q_ref, k_ref, vt_ref, qi_ref, kia_ref, kib_ref, sm_ref, ubuf_ref,
                    *, dm, tiles_per_seq):
    tm = x_ref.shape[0]

    @pl.when(pl.program_id(0) % tiles_per_seq == 0)
    def _():
        ubuf_ref[0:SUBLANES, :] = jnp.zeros((SUBLANES, ubuf_ref.shape[1]), F32)

    x = x_ref[...]
    ms = jnp.mean(x * x, axis=-1, keepdims=True)
    xn = (x * lax.rsqrt(ms + EPS) * gmix_ref[...]).astype(BF16)

    off = [0]

    def seg(width):
        lo = off[0]
        off[0] = lo + width
        return jnp.dot(xn, w_ref[:, lo:lo + width], preferred_element_type=F32)

    gz_ref[...] = _silu(seg(dm["ssd_w"]))

    u = seg(dm["conv_ch"])
    ubuf_ref[SUBLANES:SUBLANES + tm, :] = u

    def head_norm(h, gain, n_heads, out_ref):
        for i in range(n_heads):
            hh = h[:, i * ATT_HEAD_DIM:(i + 1) * ATT_HEAD_DIM]
            m2 = jnp.mean(hh * hh, axis=-1, keepdims=True)
            out_ref[:, i * ATT_HEAD_DIM:(i + 1) * ATT_HEAD_DIM] = (hh * lax.rsqrt(m2 + EPS) * gain).astype(out_ref.dtype)

    head_norm(seg(dm["n_q"]), gq_ref[...] * (ATT_HEAD_DIM ** -0.5 * LOG2_E), dm["att_heads"], q_ref)
    head_norm(seg(dm["n_kv"]), gk_ref[...], ATT_KV_HEADS, k_ref)
    vt_ref[0] = seg(dm["n_kv"]).T.astype(BF16)
    qi_ref[...] = (seg(dm["n_qi"]) * (IDX_HEAD_DIM ** -0.5)).astype(BF16)

    lane = lax.broadcasted_iota(jnp.int32, (1, LANES), 1)
    tail = seg(2 * LANES)
    kk = tail[:, :LANES]
    m2 = jnp.mean(kk * kk, axis=-1, keepdims=True)
    kin = kk * lax.rsqrt(m2 + EPS) * gki_ref[...]
    kia_ref[...] = jnp.where(lane < IDX_HEAD_DIM, kin, 0.0).astype(BF16)
    kib_ref[...] = jnp.where(lane >= IDX_HEAD_DIM, kin, 0.0).astype(BF16)

    sm = tail[:, LANES:]
    wi = sm * (IDX_HEADS ** -0.5)
    dt = _softplus(sm + dtb_ref[...])
    sm_ref[...] = jnp.where(lane < DT_LO, wi, jnp.where(lane < DT_LO + dm["ssd_heads"], dt, 0.0))

    acc = cb_ref[...] + cw_ref[SSD_CONV - 1:SSD_CONV, :] * u
    for s in range(1, SSD_CONV):
        acc = acc + cw_ref[SSD_CONV - 1 - s:SSD_CONV - s, :] * ubuf_ref[SUBLANES - s:SUBLANES - s + tm, :]
    ubuf_ref[0:SUBLANES, :] = ubuf_ref[tm:tm + SUBLANES, :]
    xc_ref[...] = _silu(acc)


def _in_proj(x2, gmix, w_all, gq, gk, gki2, dtb, conv_w, conv_b, *, layer, dm, tm, seq):
    t, d = x2.shape
    assert seq % tm == 0 and t % tm == 0
    nw = w_all.shape[2]
    n_kv = dm["n_kv"]
    row = lambda width: pl.BlockSpec((tm, width), lambda i: (i, 0))
    const = lambda shape: pl.BlockSpec(shape, lambda i: (0,) * len(shape))
    out_shape = (
        jax.ShapeDtypeStruct((t, dm["ssd_w"]), F32),
        jax.ShapeDtypeStruct((t, dm["conv_ch"]), F32),
        jax.ShapeDtypeStruct((t, dm["n_q"]), BF16),
        jax.ShapeDtypeStruct((t, n_kv), BF16),
        jax.ShapeDtypeStruct((t // tm, n_kv, tm), BF16),
        jax.ShapeDtypeStruct((t, dm["n_qi"]), BF16),
        jax.ShapeDtypeStruct((t, LANES), BF16),
        jax.ShapeDtypeStruct((t, LANES), BF16),
        jax.ShapeDtypeStruct((t, LANES), F32),
    )
    out_specs = (
        row(dm["ssd_w"]), row(dm["conv_ch"]), row(dm["n_q"]), row(n_kv),
        pl.BlockSpec((1, n_kv, tm), lambda i: (i, 0, 0)),
        row(dm["n_qi"]), row(LANES), row(LANES), row(LANES),
    )
    return pl.pallas_call(
        functools.partial(_in_proj_kernel, dm=dm, tiles_per_seq=seq // tm),
        grid=(t // tm,),
        in_specs=[
            row(d), const((1, d)),
            pl.BlockSpec((None, d, nw), lambda i: (layer, 0, 0), pipeline_mode=pl.Buffered(1)),
            const((1, ATT_HEAD_DIM)), const((1, ATT_HEAD_DIM)), const((1, LANES)), const((1, LANES)),
            const((SSD_CONV, dm["conv_ch"])), const((1, dm["conv_ch"])),
        ],
        out_specs=out_specs,
        out_shape=out_shape,
        scratch_shapes=[pltpu.VMEM((tm + SUBLANES, dm["conv_ch"]), F32)],
        compiler_params=pltpu.CompilerParams(dimension_semantics=("arbitrary",), vmem_limit_bytes=VMEM_LIMIT),
        name="in_proj",
    )(x2, gmix, w_all, gq, gk, gki2, dtb, conv_w, conv_b)


def _ssd_kernel(xc_ref, gz_ref, sm_ref, alog_ref, dskip_ref, gssd_ref, e_ref, y_ref, state_ref, *, dm):
    q = SSD_CHUNK

    @pl.when(pl.program_id(1) == 0)
    def _():
        state_ref[...] = jnp.zeros_like(state_ref)

    lane = lax.broadcasted_iota(jnp.int32, (1, LANES), 1)
    is_dt = (lane >= DT_LO) & (lane < DT_LO + dm["ssd_heads"])
    a_full = jnp.where(is_dt, -jnp.exp(alog_ref[...]), 0.0)
    row_i = lax.broadcasted_iota(jnp.int32, (q, q), 0)
    col_i = lax.broadcasted_iota(jnp.int32, (q, q), 1)
    causal = row_i >= col_i
    tri = causal.astype(F32)

    for r0 in range(0, xc_ref.shape[0], q):
        _ssd_chunk(xc_ref[r0:r0 + q, :], gz_ref[r0:r0 + q, :], sm_ref[r0:r0 + q, :], a_full, is_dt, lane, causal, tri,
                   dskip_ref, gssd_ref, e_ref, y_ref.at[r0:r0 + q, :], state_ref, dm=dm)


def _ssd_chunk(xc, gz, sm, a_full, is_dt, lane, causal, tri, dskip_ref, gssd_ref, e_ref, y_ref, state_ref, *, dm):
    q = SSD_CHUNK
    ssd_w = dm["ssd_w"]
    n_bc = dm["n_bc"]
    gw = ssd_w // SSD_GROUPS
    hpg = dm["ssd_heads"] // SSD_GROUPS
    xs = xc[:, :ssd_w]
    bm = xc[:, ssd_w:ssd_w + n_bc]
    cm = xc[:, ssd_w + n_bc:ssd_w + 2 * n_bc]
    dt_full = jnp.where(is_dt, sm, 0.0)
    adt = dt_full * a_full
    a_cum = jnp.dot(tri, adt, precision=HIGHEST, preferred_element_type=F32)
    a_cum_t = a_cum.T
    both = jnp.dot(_split3(jnp.concatenate([a_cum, dt_full], axis=0)), e_ref[...], preferred_element_type=F32)
    acum_x = both[:q]
    dt_x = both[q:]
    alast_x = acum_x[q - 1:q, :]
    decay_in = jnp.exp(acum_x)
    decay_out = jnp.exp(alast_x - acum_x)
    chunk_decay = jnp.exp(alast_x)

    xdt = xs * dt_x
    xdt_b = xdt.astype(BF16)
    xw_b = (xdt * decay_out).astype(BF16)

    y_diag, y_off = [], []
    for g in range(SSD_GROUPS):
        cm_g = cm[:, g * SSD_STATE:(g + 1) * SSD_STATE]
        bm_g = bm[:, g * SSD_STATE:(g + 1) * SSD_STATE]
        cm_b = cm_g.astype(BF16)
        cb = _nt_dot(cm_b, bm_g.astype(BF16))
        s_prev = state_ref[:, g * gw:(g + 1) * gw]
        y_off.append(jnp.dot(cm_b, s_prev.astype(BF16), preferred_element_type=F32))
        state_ref[:, g * gw:(g + 1) * gw] = (
            s_prev * chunk_decay[:, g * gw:(g + 1) * gw]
            + jnp.dot(bm_g.T.astype(BF16), xw_b[:, g * gw:(g + 1) * gw], preferred_element_type=F32))
        for j in range(hpg // 2):
            h0 = g * hpg + 2 * j
            xp = xdt_b[:, h0 * SSD_HEAD_DIM:(h0 + 2) * SSD_HEAD_DIM]
            pair = None
            for r in range(2):
                c = DT_LO + h0 + r
                seg = a_cum[:, c:c + 1] - a_cum_t[c:c + 1, :]
                decay = jnp.exp(jnp.where(causal, seg, -jnp.inf))
                m = (cb * decay).astype(BF16)
                keep = (lane < SSD_HEAD_DIM) if r == 0 else (lane >= SSD_HEAD_DIM)
                part = jnp.dot(m, jnp.where(keep, xp, jnp.zeros_like(xp)), preferred_element_type=F32)
                pair = part if pair is None else pair + part
            y_diag.append(pair)

    y = jnp.concatenate(y_diag, axis=1) + decay_in * jnp.concatenate(y_off, axis=1) + dskip_ref[...] * xs
    y = y * gz
    outs = []
    for g in range(SSD_GROUPS):
        yg = y[:, g * gw:(g + 1) * gw]
        outs.append(yg * lax.rsqrt(jnp.mean(yg * yg, axis=-1, keepdims=True) + EPS))
    y_ref[...] = (jnp.concatenate(outs, axis=1) * gssd_ref[...]).astype(y_ref.dtype)


def _ssd(xc, gz, sm, alog_pad, dskip_x, g_ssd, e_mat, *, dm, batch, seq):
    t = xc.shape[0]
    rows = SSD_STEP_CHUNKS * SSD_CHUNK
    assert seq % rows == 0
    nc = seq // rows
    cc = dm["conv_ch"]
    ssd_w = dm["ssd_w"]
    row = lambda width: pl.BlockSpec((rows, width), lambda b, c: (b * nc + c, 0))
    const = lambda shape: pl.BlockSpec(shape, lambda b, c: (0,) * len(shape))
    return pl.pallas_call(
        functools.partial(_ssd_kernel, dm=dm),
        grid=(batch, nc),
        in_specs=[row(cc), row(ssd_w), row(LANES),
                  const((1, LANES)), const((1, ssd_w)), const((1, ssd_w)), const((3 * LANES, ssd_w))],
        out_specs=row(ssd_w),
        out_shape=jax.ShapeDtypeStruct((t, ssd_w), BF16),
        scratch_shapes=[pltpu.VMEM((SSD_STATE, ssd_w), F32)],
        compiler_params=pltpu.CompilerParams(dimension_semantics=("arbitrary", "arbitrary"), vmem_limit_bytes=VMEM_LIMIT),
        name="ssd",
    )(xc, gz, sm, alog_pad, dskip_x, g_ssd, e_mat)


def _key_to_f32(ks):
    return lax.bitcast_convert_type(ks ^ ((ks >> 31) & jnp.int32(0x7FFFFFFF)), F32)


COUNT_ACCS = 4


def _count_rows(m, cnt):
    cnt = list(cnt)
    for j, r in enumerate(range(0, m.shape[0], SUBLANES)):
        a = j % COUNT_ACCS
        cnt[a] = jnp.where(m[r:r + SUBLANES, :], cnt[a] + 1, cnt[a])
    return tuple(cnt)


def _dsa_kernel(q_ref, k_ref, vt_ref, qi_ref, kia_ref, kib_ref, sm_ref, o_ref,
                score_ref, bias_ref, thr_ref, mstar_ref, m_ref, acc_ref, logit_ref, cmax_ref, *, dm, seq, qb):
    kc = qb
    topk = dm["topk"]
    n_heads = dm["att_heads"]
    rep = n_heads // ATT_KV_HEADS
    i = pl.program_id(1)
    nkc = i + 1
    q_pos = i * qb + lax.broadcasted_iota(jnp.int32, (1, qb), 1)

    def key_pos(c):
        return c * kc + lax.broadcasted_iota(jnp.int32, (kc, 1), 0)

    def rows(c):
        return pl.ds(pl.multiple_of(c * kc, kc), kc)

    wi_t = sm_ref[...].T
    qi = qi_ref[...]

    def score_chunk(c):
        ka = kia_ref[rows(c), :]
        kb = kib_ref[rows(c), :]
        acc = jnp.zeros((kc, qb), F32)
        for j in range(IDX_HEADS // 2):
            qp = qi[:, j * LANES:(j + 1) * LANES]
            w0 = wi_t[WI_LO + 2 * j:WI_LO + 2 * j + 1, :]
            w1 = wi_t[WI_LO + 2 * j + 1:WI_LO + 2 * j + 2, :]
            acc = acc + w0 * jnp.maximum(_nt_dot(ka, qp), 0.0) + w1 * jnp.maximum(_nt_dot(kb, qp), 0.0)
        score_ref[rows(c), :] = jnp.where(key_pos(c) <= q_pos, acc, -jnp.inf)

    def pairwise(one_chunk):
        def two(j, carry):
            one_chunk(2 * j)
            one_chunk(2 * j + 1)
            return carry

        lax.fori_loop(0, nkc // 2, two, 0)

        @pl.when(nkc % 2 == 1)
        def _():
            one_chunk(nkc - 1)

    pairwise(score_chunk)

    def count(pred):
        def body(c, cnt):
            return _count_rows(pred(score_ref[rows(c), :], c), cnt)
        cnt = lax.fori_loop(0, nkc, body, (jnp.zeros((SUBLANES, qb), jnp.int32),) * COUNT_ACCS)
        return jnp.sum(functools.reduce(lambda a, b: a + b, cnt), axis=0, keepdims=True)

    def bit_step(p, carry):
        res, n_res = carry
        cand = res | (jnp.int32(1) << (31 - p))
        cand_s = cand ^ jnp.int32(INT_MIN)
        cand_f = _key_to_f32(cand_s)
        n_cand = count(lambda sc, c: sc >= cand_f)
        ok = (n_cand >= topk) | (cand_s <= jnp.int32(KEY_NEG_INF))
        return jnp.where(ok, cand, res), jnp.where(ok, n_cand, n_res)

    zeros_q = jnp.zeros((1, qb), jnp.int32)
    res, n_ge = lax.fori_loop(0, 32, bit_step, (zeros_q, zeros_q))
    thr_s = res ^ jnp.int32(INT_MIN)
    thr0 = _key_to_f32(thr_s)
    thr_ref[...] = thr0
    mstar_ref[...] = jnp.full((1, qb), seq, jnp.int32)

    @pl.when(jnp.max(n_ge) > topk)
    def _():
        def refine(_, lh):
            lo, hi = lh
            mid = lo + 0.5 * (hi - lo)
            ok = count(lambda sc, c: sc >= mid) >= topk
            return jnp.where(ok, mid, lo), jnp.where(ok, hi, mid)

        thr, _ = lax.fori_loop(0, REFINE_STEPS, refine, (thr0, _key_to_f32(thr_s + 1)))
        thr_ref[...] = thr
        need = topk - count(lambda sc, c: sc > thr)
        nbits = max(1, int(np.ceil(np.log2(seq))))

        def idx_step(p, res):
            cand = res | (jnp.int32(1) << (nbits - 1 - p))
            n_eq_before = count(lambda sc, c: (sc == thr) & (key_pos(c) < cand))
            return jnp.where(n_eq_before < need, cand, res)

        mstar_ref[...] = lax.fori_loop(0, nbits, idx_step, jnp.zeros((1, qb), jnp.int32))

    thr = thr_ref[...]
    mstar = mstar_ref[...]

    def bias_chunk(c, carry):
        sc = score_ref[rows(c), :]
        kp = key_pos(c)
        sel = ((sc > thr) | ((sc == thr) & (kp <= mstar))) & (kp <= q_pos)
        bias_ref[rows(c), :] = jnp.where(sel, 0.0, NEG_BIG)
        return carry

    lax.fori_loop(0, nkc, bias_chunk, 0)

    m_ref[...] = jnp.full(m_ref.shape, NEG_BIG, F32)
    acc_ref[...] = jnp.zeros(acc_ref.shape, F32)

    ones_rows = jnp.ones((SUM_ROWS, kc), BF16)

    def logits_phase(c):
        slot = (c % 2) * n_heads
        for h in range(n_heads):
            g = h // rep
            kch = k_ref[rows(c), g * ATT_HEAD_DIM:(g + 1) * ATT_HEAD_DIM]
            logit = _nt_dot(kch, q_ref[:, h * ATT_HEAD_DIM:(h + 1) * ATT_HEAD_DIM]) + bias_ref[rows(c), :]
            logit_ref[slot + h] = logit
            cmax_ref[slot + h, 0:1, :] = jnp.max(logit, axis=0, keepdims=True)

    def softmax_phase(c):
        slot = (c % 2) * n_heads
        for h in range(n_heads):
            g = h // rep
            vt = jnp.concatenate([vt_ref[c, g * ATT_HEAD_DIM:(g + 1) * ATT_HEAD_DIM, :], ones_rows], axis=0)
            m_old = m_ref[h:h + 1, :]
            m_new = jnp.maximum(m_old, cmax_ref[slot + h, 0:1, :])
            p = jnp.exp2(logit_ref[slot + h] - m_new)
            m_ref[h:h + 1, :] = m_new
            acc_ref[h] = (acc_ref[h] * jnp.exp2(m_old - m_new)
                          + jnp.dot(vt, p.astype(BF16), preferred_element_type=F32))

    def att_chunk(c):
        logits_phase(c)
        softmax_phase(c)

    pairwise(att_chunk)
    for h in range(n_heads):
        acc = acc_ref[h]
        o = acc[:ATT_HEAD_DIM, :] / acc[ATT_HEAD_DIM:ATT_HEAD_DIM + 1, :]
        o_ref[:, h * ATT_HEAD_DIM:(h + 1) * ATT_HEAD_DIM] = o.T.astype(o_ref.dtype)


def _dsa(q, k, vt, qi, kia, kib, sm, *, dm, batch, seq, qb):
    t = q.shape[0]
    nqb = seq // qb
    n_kv = dm["n_kv"]
    assert qb >= dm["topk"] and seq % qb == 0
    qrow = lambda width: pl.BlockSpec((qb, width), lambda b, i: (b * nqb + i, 0))
    brow = lambda width: pl.BlockSpec((seq, width), lambda b, i: (b, 0))
    return pl.pallas_call(
        functools.partial(_dsa_kernel, dm=dm, seq=seq, qb=qb),
        grid=(batch, nqb),
        in_specs=[qrow(dm["n_q"]), brow(n_kv),
                  pl.BlockSpec((nqb, n_kv, qb), lambda b, i: (b, 0, 0)),
                  qrow(dm["n_qi"]), brow(LANES), brow(LANES), qrow(LANES)],
        out_specs=qrow(dm["n_q"]),
        out_shape=jax.ShapeDtypeStruct((t, dm["n_q"]), BF16),
        scratch_shapes=[pltpu.VMEM((seq, qb), F32), pltpu.VMEM((seq, qb), F32),
                        pltpu.VMEM((1, qb), F32), pltpu.VMEM((1, qb), jnp.int32),
                        pltpu.VMEM((dm["att_heads"], qb), F32),
                        pltpu.VMEM((dm["att_heads"], ATT_HEAD_DIM + SUM_ROWS, qb), F32),
                        pltpu.VMEM((2 * dm["att_heads"], qb, qb), F32),
                        pltpu.VMEM((2 * dm["att_heads"], SUBLANES, qb), F32)],
        compiler_params=pltpu.CompilerParams(dimension_semantics=("arbitrary", "arbitrary"), vmem_limit_bytes=VMEM_LIMIT),
        name="dsa",
    )(q, k, vt, qi, kia, kib, sm)


def _out_proj_kernel(x_ref, ya_ref, yb_ref, w_ref, o_ref):
    na = ya_ref.shape[1]
    o_ref[...] = (x_ref[...]
                  + jnp.dot(ya_ref[...], w_ref[:na, :], preferred_element_type=F32)
                  + jnp.dot(yb_ref[...], w_ref[na:, :], preferred_element_type=F32))


def _out_proj(x2, y_ssd, y_att, w_out, *, layer, tm):
    t, d = x2.shape
    row = lambda width: pl.BlockSpec((tm, width), lambda i: (i, 0))
    return pl.pallas_call(
        _out_proj_kernel,
        grid=(t // tm,),
        in_specs=[row(d), row(y_ssd.shape[1]), row(y_att.shape[1]),
                  pl.BlockSpec((None,) + w_out.shape[1:], lambda i: (layer, 0, 0), pipeline_mode=pl.Buffered(1))],
        out_specs=row(d),
        out_shape=jax.ShapeDtypeStruct((t, d), F32),
        compiler_params=pltpu.CompilerParams(dimension_semantics=("arbitrary",), vmem_limit_bytes=VMEM_LIMIT),
        name="out_proj",
    )(x2, y_ssd, y_att, w_out)


def _mlp_kernel(x_ref, g_ref, wu_ref, wd_ref, o_ref, xn_ref, acc_ref):
    f = pl.program_id(1)

    @pl.when(f == 0)
    def _():
        x = x_ref[...]
        ms = jnp.mean(x * x, axis=-1, keepdims=True)
        xn_ref[...] = (x * lax.rsqrt(ms + EPS) * g_ref[...]).astype(BF16)
        acc_ref[...] = jnp.zeros_like(acc_ref)

    u = jnp.maximum(jnp.dot(xn_ref[...], wu_ref[...], preferred_element_type=F32), 0.0)
    acc_ref[...] += jnp.dot((u * u).astype(BF16), wd_ref[...], preferred_element_type=F32)

    @pl.when(f == pl.num_programs(1) - 1)
    def _():
        o_ref[...] = x_ref[...] + acc_ref[...]


def _mlp(x2, g_mlp, w_up, w_down, *, layer, tm, tf):
    t, d = x2.shape
    d_ff = w_up.shape[2]
    return pl.pallas_call(
        _mlp_kernel,
        grid=(t // tm, d_ff // tf),
        in_specs=[pl.BlockSpec((tm, d), lambda i, f: (i, 0)),
                  pl.BlockSpec((1, d), lambda i, f: (0, 0)),
                  pl.BlockSpec((None, d, tf), lambda i, f: (layer, 0, f)),
                  pl.BlockSpec((None, tf, d), lambda i, f: (layer, f, 0))],
        out_specs=pl.BlockSpec((tm, d), lambda i, f: (i, 0)),
        out_shape=jax.ShapeDtypeStruct((t, d), F32),
        scratch_shapes=[pltpu.VMEM((tm, d), BF16), pltpu.VMEM((tm, d), F32)],
        compiler_params=pltpu.CompilerParams(dimension_semantics=("arbitrary", "arbitrary"), vmem_limit_bytes=VMEM_LIMIT),
        name="mlp",
    )(x2, g_mlp, w_up, w_down)


def _pack_w_in(w_in, dm):
    sizes = (dm["ssd_w"], dm["conv_ch"], dm["ssd_heads"], dm["n_q"], dm["n_kv"], dm["n_kv"], dm["n_qi"],
             IDX_HEAD_DIM, IDX_HEADS)
    pts = [int(v) for v in np.cumsum(sizes)[:-1]]
    z, xbc, dt, q, k, v, qi, ki, wi = jnp.split(w_in, pts, axis=2)
    pad = jnp.zeros(w_in.shape[:2] + (LANES - IDX_HEADS - dm["ssd_heads"],), w_in.dtype)
    return jnp.concatenate([z, xbc, q, k, v, qi, ki, ki, wi, dt, pad], axis=2).astype(BF16)


def _expand_matrix(dm):
    e = np.zeros((LANES, dm["ssd_w"]), np.float32)
    for h in range(dm["ssd_heads"]):
        e[DT_LO + h, h * SSD_HEAD_DIM:(h + 1) * SSD_HEAD_DIM] = 1.0
    return jnp.asarray(np.concatenate([e, e, e], axis=0)).astype(BF16)


def _lane_pad(v, lo):
    return jnp.zeros((1, LANES), F32).at[0, lo:lo + v.shape[0]].set(v)


def kernel(x, g_mix, w_in, conv_w, conv_b, dt_bias, a_log, d_skip, g_ssd, g_q, g_k, g_kidx, w_out, g_mlp, w_up, w_down):
    batch, seq, d = x.shape
    depth = w_in.shape[0]
    dm = _dims(d, seq)
    assert dm["ssd_heads"] + IDX_HEADS <= LANES and (dm["ssd_heads"] // SSD_GROUPS) % 2 == 0
    assert seq % SSD_CHUNK == 0 and dm["att_heads"] % ATT_KV_HEADS == 0
    qb = 256
    tm = min(512, batch * seq)
    tf = min(1024, w_up.shape[2])
    assert seq % qb == 0 and (batch * seq) % tm == 0 and w_up.shape[2] % tf == 0
    e_mat = _expand_matrix(dm)
    w_all = _pack_w_in(w_in, dm)
    w_out, w_up, w_down = (w.astype(BF16) for w in (w_out, w_up, w_down))

    x2 = x.reshape(batch * seq, d)
    for i in range(depth):
        gz, xc, q, k, vt, qi, kia, kib, sm = _in_proj(
            x2, g_mix[i][None, :], w_all, g_q[i][None, :], g_k[i][None, :],
            jnp.concatenate([g_kidx[i], g_kidx[i]])[None, :], _lane_pad(dt_bias[i], DT_LO),
            conv_w[i], conv_b[i][None, :], layer=i, dm=dm, tm=qb, seq=seq)
        y_ssd = _ssd(xc, gz, sm, _lane_pad(a_log[i], DT_LO),
                     jnp.repeat(d_skip[i], SSD_HEAD_DIM)[None, :], g_ssd[i][None, :], e_mat,
                     dm=dm, batch=batch, seq=seq)
        y_att = _dsa(q, k, vt, qi, kia, kib, sm, dm=dm, batch=batch, seq=seq, qb=qb)
        x2 = _out_proj(x2, y_ssd, y_att, w_out, layer=i, tm=tm)
        x2 = _mlp(x2, g_mlp[i][None, :], w_up, w_down, layer=i, tm=tm, tf=tf)
    return x2.reshape(batch, seq, d)
```

```python
import functools

import numpy as np
import jax
import jax.numpy as jnp
from jax import lax
from jax.experimental import pallas as pl
from jax.experimental.pallas import tpu as pltpu

EPS = 1e-6
SSD_HEAD_DIM = 64
SSD_GROUPS = 2
SSD_STATE = 128
SSD_CONV = 4
SSD_CHUNK = 128
ATT_HEAD_DIM = 128
ATT_KV_HEADS = 4
IDX_HEADS = 16
IDX_HEAD_DIM = 64
IDX_TOPK_MAX = 256

LANES = 128
SUBLANES = 8
VMEM_LIMIT = 56 * 1024 * 1024

WI_LO, DT_LO = 0, 16
NEG_BIG = -1e30
INT_MIN = -(2 ** 31)
KEY_NEG_INF = 0x807FFFFF - 2 ** 32
REFINE_STEPS = 30
LOG2_E = 1.4426950408889634
SUM_ROWS = 16

F32 = jnp.float32
BF16 = jnp.bfloat16
HIGHEST = lax.Precision.HIGHEST


def _dims(d_model, seq):
    ssd_w = d_model // 2
    att_w = d_model - ssd_w
    n_bc = SSD_GROUPS * SSD_STATE
    return dict(
        ssd_w=ssd_w, ssd_heads=ssd_w // SSD_HEAD_DIM, n_bc=n_bc, conv_ch=ssd_w + 2 * n_bc,
        n_q=att_w, att_heads=att_w // ATT_HEAD_DIM, n_kv=ATT_KV_HEADS * ATT_HEAD_DIM,
        n_qi=IDX_HEADS * IDX_HEAD_DIM, topk=min(IDX_TOPK_MAX, seq // 4))


def _nt_dot(a, b):
    return lax.dot_general(a, b, (((1,), (1,)), ((), ())), preferred_element_type=F32)


def _silu(v):
    return v * (1.0 / (1.0 + jnp.exp(-v)))


def _softplus(v):
    return jnp.maximum(v, 0.0) + jnp.log1p(jnp.exp(-jnp.abs(v)))


def _split3(a):
    hi = a.astype(BF16)
    rest = a - hi.astype(F32)
    mid = rest.astype(BF16)
    lo = (rest - mid.astype(F32)).astype(BF16)
    return jnp.concatenate([hi, mid, lo], axis=1)


def _in_proj_kernel(x_ref, gmix_ref, w_ref, gq_ref, gk_ref, gki_ref, dtb_ref, cw_ref, cb_ref,
                    alog_ref, dskip_ref, gssd_ref, e_ref,
                    y_ref, q_ref, k_ref, vt_ref, qi_ref, kia_ref, kib_ref, sm_ref, ubuf_ref, state_ref,
                    *, dm, tiles_per_seq):
    tm = x_ref.shape[0]

    @pl.when(pl.program_id(0) % tiles_per_seq == 0)
    def _():
        ubuf_ref[0:SUBLANES, :] = jnp.zeros((SUBLANES, ubuf_ref.shape[1]), F32)
        state_ref[...] = jnp.zeros_like(state_ref)

    x = x_ref[...]
    ms = jnp.mean(x * x, axis=-1, keepdims=True)
    xn = (x * lax.rsqrt(ms + EPS) * gmix_ref[...]).astype(BF16)

    off = [0]

    def seg(width):
        lo = off[0]
        off[0] = lo + width
        return jnp.dot(xn, w_ref[:, lo:lo + width], preferred_element_type=F32)

    u = seg(dm["conv_ch"])
    ubuf_ref[SUBLANES:SUBLANES + tm, :] = u
    lane = lax.broadcasted_iota(jnp.int32, (1, LANES), 1)
    tail = seg(2 * LANES)
    gz = _silu(seg(dm["ssd_w"]))

    def head_norm(h, gain, n_heads, out_ref):
        for i in range(n_heads):
            hh = h[:, i * ATT_HEAD_DIM:(i + 1) * ATT_HEAD_DIM]
            m2 = jnp.mean(hh * hh, axis=-1, keepdims=True)
            out_ref[:, i * ATT_HEAD_DIM:(i + 1) * ATT_HEAD_DIM] = (hh * lax.rsqrt(m2 + EPS) * gain).astype(out_ref.dtype)

    head_norm(seg(dm["n_q"]), gq_ref[...] * (ATT_HEAD_DIM ** -0.5 * LOG2_E), dm["att_heads"], q_ref)
    head_norm(seg(dm["n_kv"]), gk_ref[...], ATT_KV_HEADS, k_ref)
    vt_ref[0] = seg(dm["n_kv"]).T.astype(BF16)
    qi_ref[...] = (seg(dm["n_qi"]) * (IDX_HEAD_DIM ** -0.5)).astype(BF16)

    kk = tail[:, :LANES]
    m2 = jnp.mean(kk * kk, axis=-1, keepdims=True)
    kin = kk * lax.rsqrt(m2 + EPS) * gki_ref[...]
    kia_ref[...] = jnp.where(lane < IDX_HEAD_DIM, kin, 0.0).astype(BF16)
    kib_ref[...] = jnp.where(lane >= IDX_HEAD_DIM, kin, 0.0).astype(BF16)

    sm = tail[:, LANES:]
    wi = sm * (IDX_HEADS ** -0.5)
    dt = _softplus(sm + dtb_ref[...])
    sm_val = jnp.where(lane < DT_LO, wi, jnp.where(lane < DT_LO + dm["ssd_heads"], dt, 0.0))
    sm_ref[...] = sm_val

    acc = cb_ref[...] + cw_ref[SSD_CONV - 1:SSD_CONV, :] * u
    for s in range(1, SSD_CONV):
        acc = acc + cw_ref[SSD_CONV - 1 - s:SSD_CONV - s, :] * ubuf_ref[SUBLANES - s:SUBLANES - s + tm, :]
    ubuf_ref[0:SUBLANES, :] = ubuf_ref[tm:tm + SUBLANES, :]
    xc = _silu(acc)

    q = SSD_CHUNK
    is_dt = (lane >= DT_LO) & (lane < DT_LO + dm["ssd_heads"])
    a_full = jnp.where(is_dt, -jnp.exp(alog_ref[...]), 0.0)
    causal = lax.broadcasted_iota(jnp.int32, (q, q), 0) >= lax.broadcasted_iota(jnp.int32, (q, q), 1)
    tri = causal.astype(F32)
    for r0 in range(0, tm, q):
        _ssd_chunk(xc[r0:r0 + q, :], gz[r0:r0 + q, :], sm_val[r0:r0 + q, :], a_full, is_dt, lane, causal, tri,
                   dskip_ref, gssd_ref, e_ref, y_ref.at[r0:r0 + q, :], state_ref, dm=dm)


def _in_proj(x2, gmix, w_all, gq, gk, gki2, dtb, conv_w, conv_b, alog_pad, dskip_x, g_ssd, e_mat,
             *, layer, dm, tm, seq):
    t, d = x2.shape
    assert seq % tm == 0 and t % tm == 0 and tm % SSD_CHUNK == 0
    nw = w_all.shape[2]
    n_kv = dm["n_kv"]
    row = lambda width: pl.BlockSpec((tm, width), lambda i: (i, 0))
    const = lambda shape: pl.BlockSpec(shape, lambda i: (0,) * len(shape))
    out_shape = (
        jax.ShapeDtypeStruct((t, dm["ssd_w"]), BF16),
        jax.ShapeDtypeStruct((t, dm["n_q"]), BF16),
        jax.ShapeDtypeStruct((t, n_kv), BF16),
        jax.ShapeDtypeStruct((t // tm, n_kv, tm), BF16),
        jax.ShapeDtypeStruct((t, dm["n_qi"]), BF16),
        jax.ShapeDtypeStruct((t, LANES), BF16),
        jax.ShapeDtypeStruct((t, LANES), BF16),
        jax.ShapeDtypeStruct((t, LANES), F32),
    )
    out_specs = (
        row(dm["ssd_w"]), row(dm["n_q"]), row(n_kv),
        pl.BlockSpec((1, n_kv, tm), lambda i: (i, 0, 0)),
        row(dm["n_qi"]), row(LANES), row(LANES), row(LANES),
    )
    return pl.pallas_call(
        functools.partial(_in_proj_kernel, dm=dm, tiles_per_seq=seq // tm),
        grid=(t // tm,),
        in_specs=[
            row(d), const((1, d)),
            pl.BlockSpec((None, d, nw), lambda i: (layer, 0, 0), pipeline_mode=pl.Buffered(1)),
            const((1, ATT_HEAD_DIM)), const((1, ATT_HEAD_DIM)), const((1, LANES)), const((1, LANES)),
            const((SSD_CONV, dm["conv_ch"])), const((1, dm["conv_ch"])),
            const((1, LANES)), const((1, dm["ssd_w"])), const((1, dm["ssd_w"])), const((3 * LANES, dm["ssd_w"])),
        ],
        out_specs=out_specs,
        out_shape=out_shape,
        scratch_shapes=[pltpu.VMEM((tm + SUBLANES, dm["conv_ch"]), F32), pltpu.VMEM((SSD_STATE, dm["ssd_w"]), F32)],
        compiler_params=pltpu.CompilerParams(dimension_semantics=("arbitrary",), vmem_limit_bytes=VMEM_LIMIT),
        name="in_proj",
    )(x2, gmix, w_all, gq, gk, gki2, dtb, conv_w, conv_b, alog_pad, dskip_x, g_ssd, e_mat)


def _ssd_chunk(xc, gz, sm, a_full, is_dt, lane, causal, tri, dskip_ref, gssd_ref, e_ref, y_ref, state_ref, *, dm):
    q = SSD_CHUNK
    ssd_w = dm["ssd_w"]
    n_bc = dm["n_bc"]
    gw = ssd_w // SSD_GROUPS
    hpg = dm["ssd_heads"] // SSD_GROUPS
    xs = xc[:, :ssd_w]
    bm = xc[:, ssd_w:ssd_w + n_bc]
    cm = xc[:, ssd_w + n_bc:ssd_w + 2 * n_bc]
    dt_full = jnp.where(is_dt, sm, 0.0)
    adt = dt_full * a_full
    a_cum = jnp.dot(tri, adt, precision=HIGHEST, preferred_element_type=F32)
    a_cum_t = a_cum.T
    both = jnp.dot(_split3(jnp.concatenate([a_cum, dt_full], axis=0)), e_ref[...], preferred_element_type=F32)
    acum_x = both[:q]
    dt_x = both[q:]
    alast_x = acum_x[q - 1:q, :]
    decay_in = jnp.exp(acum_x)
    decay_out = jnp.exp(alast_x - acum_x)
    chunk_decay = jnp.exp(alast_x)

    xdt = xs * dt_x
    xdt_b = xdt.astype(BF16)
    xw_b = (xdt * decay_out).astype(BF16)

    y_diag, y_off = [], []
    for g in range(SSD_GROUPS):
        cm_g = cm[:, g * SSD_STATE:(g + 1) * SSD_STATE]
        bm_g = bm[:, g * SSD_STATE:(g + 1) * SSD_STATE]
        cm_b = cm_g.astype(BF16)
        cb = _nt_dot(cm_b, bm_g.astype(BF16))
        s_prev = state_ref[:, g * gw:(g + 1) * gw]
        y_off.append(jnp.dot(cm_b, s_prev.astype(BF16), preferred_element_type=F32))
        state_ref[:, g * gw:(g + 1) * gw] = (
            s_prev * chunk_decay[:, g * gw:(g + 1) * gw]
            + jnp.dot(bm_g.T.astype(BF16), xw_b[:, g * gw:(g + 1) * gw], preferred_element_type=F32))
        for j in range(hpg // 2):
            h0 = g * hpg + 2 * j
            xp = xdt_b[:, h0 * SSD_HEAD_DIM:(h0 + 2) * SSD_HEAD_DIM]
            pair = None
            for r in range(2):
                c = DT_LO + h0 + r
                seg = a_cum[:, c:c + 1] - a_cum_t[c:c + 1, :]
                decay = jnp.exp(jnp.where(causal, seg, -jnp.inf))
                m = (cb * decay).astype(BF16)
                keep = (lane < SSD_HEAD_DIM) if r == 0 else (lane >= SSD_HEAD_DIM)
                part = jnp.dot(m, jnp.where(keep, xp, jnp.zeros_like(xp)), preferred_element_type=F32)
                pair = part if pair is None else pair + part
            y_diag.append(pair)

    y = jnp.concatenate(y_diag, axis=1) + decay_in * jnp.concatenate(y_off, axis=1) + dskip_ref[...] * xs
    y = y * gz
    outs = []
    for g in range(SSD_GROUPS):
        yg = y[:, g * gw:(g + 1) * gw]
        outs.append(yg * lax.rsqrt(jnp.mean(yg * yg, axis=-1, keepdims=True) + EPS))
    y_ref[...] = (jnp.concatenate(outs, axis=1) * gssd_ref[...]).astype(y_ref.dtype)


def _key_to_f32(ks):
    return lax.bitcast_convert_type(ks ^ ((ks >> 31) & jnp.int32(0x7FFFFFFF)), F32)


COUNT_ACCS = 4


def _count_rows(m, cnt):
    cnt = list(cnt)
    for j, r in enumerate(range(0, m.shape[0], SUBLANES)):
        a = j % COUNT_ACCS
        cnt[a] = jnp.where(m[r:r + SUBLANES, :], cnt[a] + 1, cnt[a])
    return tuple(cnt)


def _dsa_kernel(q_ref, k_ref, vt_ref, qi_ref, kia_ref, kib_ref, sm_ref, o_ref,
                score_ref, bias_ref, thr_ref, mstar_ref, m_ref, acc_ref, logit_ref, cmax_ref, *, dm, seq, qb):
    kc = qb
    topk = dm["topk"]
    n_heads = dm["att_heads"]
    rep = n_heads // ATT_KV_HEADS
    i = pl.program_id(1)
    nkc = i + 1
    q_pos = i * qb + lax.broadcasted_iota(jnp.int32, (1, qb), 1)

    def key_pos(c):
        return c * kc + lax.broadcasted_iota(jnp.int32, (kc, 1), 0)

    def rows(c):
        return pl.ds(pl.multiple_of(c * kc, kc), kc)

    wi_t = sm_ref[...].T
    qi = qi_ref[...]

    def score_chunk(c):
        ka = kia_ref[rows(c), :]
        kb = kib_ref[rows(c), :]
        acc = jnp.zeros((kc, qb), F32)
        for j in range(IDX_HEADS // 2):
            qp = qi[:, j * LANES:(j + 1) * LANES]
            w0 = wi_t[WI_LO + 2 * j:WI_LO + 2 * j + 1, :]
            w1 = wi_t[WI_LO + 2 * j + 1:WI_LO + 2 * j + 2, :]
            acc = acc + w0 * jnp.maximum(_nt_dot(ka, qp), 0.0) + w1 * jnp.maximum(_nt_dot(kb, qp), 0.0)
        score_ref[rows(c), :] = jnp.where(key_pos(c) <= q_pos, acc, -jnp.inf)

    def pairwise(one_chunk):
        def two(j, carry):
            one_chunk(2 * j)
            one_chunk(2 * j + 1)
            return carry

        lax.fori_loop(0, nkc // 2, two, 0)

        @pl.when(nkc % 2 == 1)
        def _():
            one_chunk(nkc - 1)

    pairwise(score_chunk)

    def count(pred):
        def body(c, cnt):
            return _count_rows(pred(score_ref[rows(c), :], c), cnt)
        cnt = lax.fori_loop(0, nkc, body, (jnp.zeros((SUBLANES, qb), jnp.int32),) * COUNT_ACCS)
        return jnp.sum(functools.reduce(lambda a, b: a + b, cnt), axis=0, keepdims=True)

    def bit_step(p, carry):
        res, n_res = carry
        cand = res | (jnp.int32(1) << (31 - p))
        cand_s = cand ^ jnp.int32(INT_MIN)
        cand_f = _key_to_f32(cand_s)
        n_cand = count(lambda sc, c: sc >= cand_f)
        ok = (n_cand >= topk) | (cand_s <= jnp.int32(KEY_NEG_INF))
        return jnp.where(ok, cand, res), jnp.where(ok, n_cand, n_res)

    zeros_q = jnp.zeros((1, qb), jnp.int32)
    res, n_ge = lax.fori_loop(0, 32, bit_step, (zeros_q, zeros_q))
    thr_s = res ^ jnp.int32(INT_MIN)
    thr0 = _key_to_f32(thr_s)
    thr_ref[...] = thr0
    mstar_ref[...] = jnp.full((1, qb), seq, jnp.int32)

    @pl.when(jnp.max(n_ge) > topk)
    def _():
        def refine(_, lh):
            lo, hi = lh
            mid = lo + 0.5 * (hi - lo)
            ok = count(lambda sc, c: sc >= mid) >= topk
            return jnp.where(ok, mid, lo), jnp.where(ok, hi, mid)

        thr, _ = lax.fori_loop(0, REFINE_STEPS, refine, (thr0, _key_to_f32(thr_s + 1)))
        thr_ref[...] = thr
        need = topk - count(lambda sc, c: sc > thr)
        nbits = max(1, int(np.ceil(np.log2(seq))))

        def idx_step(p, res):
            cand = res | (jnp.int32(1) << (nbits - 1 - p))
            n_eq_before = count(lambda sc, c: (sc == thr) & (key_pos(c) < cand))
            return jnp.where(n_eq_before < need, cand, res)

        mstar_ref[...] = lax.fori_loop(0, nbits, idx_step, jnp.zeros((1, qb), jnp.int32))

    thr = thr_ref[...]
    mstar = mstar_ref[...]

    m_ref[...] = jnp.full(m_ref.shape, NEG_BIG, F32)
    acc_ref[...] = jnp.zeros(acc_ref.shape, F32)

    ones_rows = jnp.ones((SUM_ROWS, kc), BF16)

    def logits_phase(c):
        slot = (c % 2) * n_heads
        sc = score_ref[rows(c), :]
        kp = key_pos(c)
        sel = ((sc > thr) | ((sc == thr) & (kp <= mstar))) & (kp <= q_pos)
        bias_ref[rows(c), :] = jnp.where(sel, 0.0, NEG_BIG)
        for h in range(n_heads):
            g = h // rep
            kch = k_ref[rows(c), g * ATT_HEAD_DIM:(g + 1) * ATT_HEAD_DIM]
            logit = _nt_dot(kch, q_ref[:, h * ATT_HEAD_DIM:(h + 1) * ATT_HEAD_DIM]) + bias_ref[rows(c), :]
            logit_ref[slot + h] = logit
            cmax_ref[slot + h, 0:1, :] = jnp.max(logit, axis=0, keepdims=True)

    def softmax_phase(c):
        slot = (c % 2) * n_heads
        for h in range(n_heads):
            g = h // rep
            vt = jnp.concatenate([vt_ref[c, g * ATT_HEAD_DIM:(g + 1) * ATT_HEAD_DIM, :], ones_rows], axis=0)
            m_old = m_ref[h:h + 1, :]
            m_new = jnp.maximum(m_old, cmax_ref[slot + h, 0:1, :])
            p = jnp.exp2(logit_ref[slot + h] - m_new)
            m_ref[h:h + 1, :] = m_new
            acc_ref[h] = (acc_ref[h] * jnp.exp2(m_old - m_new)
                          + jnp.dot(vt, p.astype(BF16), preferred_element_type=F32))

    def att_chunk(c):
        logits_phase(c)
        softmax_phase(c)

    pairwise(att_chunk)
    for h in range(n_heads):
        acc = acc_ref[h]
        o = acc[:ATT_HEAD_DIM, :] / acc[ATT_HEAD_DIM:ATT_HEAD_DIM + 1, :]
        o_ref[:, h * ATT_HEAD_DIM:(h + 1) * ATT_HEAD_DIM] = o.T.astype(o_ref.dtype)


def _dsa(q, k, vt, qi, kia, kib, sm, *, dm, batch, seq, qb):
    t = q.shape[0]
    nqb = seq // qb
    n_kv = dm["n_kv"]
    assert qb >= dm["topk"] and seq % qb == 0
    qrow = lambda width: pl.BlockSpec((qb, width), lambda b, i: (b * nqb + i, 0))
    brow = lambda width: pl.BlockSpec((seq, width), lambda b, i: (b, 0))
    return pl.pallas_call(
        functools.partial(_dsa_kernel, dm=dm, seq=seq, qb=qb),
        grid=(batch, nqb),
        in_specs=[qrow(dm["n_q"]), brow(n_kv),
                  pl.BlockSpec((nqb, n_kv, qb), lambda b, i: (b, 0, 0)),
                  qrow(dm["n_qi"]), brow(LANES), brow(LANES), qrow(LANES)],
        out_specs=qrow(dm["n_q"]),
        out_shape=jax.ShapeDtypeStruct((t, dm["n_q"]), BF16),
        scratch_shapes=[pltpu.VMEM((seq, qb), F32), pltpu.VMEM((seq, qb), F32),
                        pltpu.VMEM((1, qb), F32), pltpu.VMEM((1, qb), jnp.int32),
                        pltpu.VMEM((dm["att_heads"], qb), F32),
                        pltpu.VMEM((dm["att_heads"], ATT_HEAD_DIM + SUM_ROWS, qb), F32),
                        pltpu.VMEM((2 * dm["att_heads"], qb, qb), F32),
                        pltpu.VMEM((2 * dm["att_heads"], SUBLANES, qb), F32)],
        compiler_params=pltpu.CompilerParams(dimension_semantics=("arbitrary", "arbitrary"), vmem_limit_bytes=VMEM_LIMIT),
        name="dsa",
    )(q, k, vt, qi, kia, kib, sm)


def _out_proj_kernel(x_ref, ya_ref, yb_ref, w_ref, o_ref):
    na = ya_ref.shape[1]
    o_ref[...] = (x_ref[...]
                  + jnp.dot(ya_ref[...], w_ref[:na, :], preferred_element_type=F32)
                  + jnp.dot(yb_ref[...], w_ref[na:, :], preferred_element_type=F32))


def _out_proj(x2, y_ssd, y_att, w_out, *, layer, tm):
    t, d = x2.shape
    row = lambda width: pl.BlockSpec((tm, width), lambda i: (i, 0))
    return pl.pallas_call(
        _out_proj_kernel,
        grid=(t // tm,),
        in_specs=[row(d), row(y_ssd.shape[1]), row(y_att.shape[1]),
                  pl.BlockSpec((None,) + w_out.shape[1:], lambda i: (layer, 0, 0), pipeline_mode=pl.Buffered(1))],
        out_specs=row(d),
        out_shape=jax.ShapeDtypeStruct((t, d), F32),
        compiler_params=pltpu.CompilerParams(dimension_semantics=("arbitrary",), vmem_limit_bytes=VMEM_LIMIT),
        name="out_proj",
    )(x2, y_ssd, y_att, w_out)


def _mlp_kernel(x_ref, g_ref, wu_ref, wd_ref, o_ref, xn_ref, acc_ref):
    f = pl.program_id(1)

    @pl.when(f == 0)
    def _():
        x = x_ref[...]
        ms = jnp.mean(x * x, axis=-1, keepdims=True)
        xn_ref[...] = (x * lax.rsqrt(ms + EPS) * g_ref[...]).astype(BF16)
        acc_ref[...] = jnp.zeros_like(acc_ref)

    u = jnp.maximum(jnp.dot(xn_ref[...], wu_ref[...], preferred_element_type=F32), 0.0)
    acc_ref[...] += jnp.dot((u * u).astype(BF16), wd_ref[...], preferred_element_type=F32)

    @pl.when(f == pl.num_programs(1) - 1)
    def _():
        o_ref[...] = x_ref[...] + acc_ref[...]


def _mlp(x2, g_mlp, w_up, w_down, *, layer, tm, tf):
    t, d = x2.shape
    d_ff = w_up.shape[2]
    return pl.pallas_call(
        _mlp_kernel,
        grid=(t // tm, d_ff // tf),
        in_specs=[pl.BlockSpec((tm, d), lambda i, f: (i, 0)),
                  pl.BlockSpec((1, d), lambda i, f: (0, 0)),
                  pl.BlockSpec((None, d, tf), lambda i, f: (layer, 0, f)),
                  pl.BlockSpec((None, tf, d), lambda i, f: (layer, f, 0))],
        out_specs=pl.BlockSpec((tm, d), lambda i, f: (i, 0)),
        out_shape=jax.ShapeDtypeStruct((t, d), F32),
        scratch_shapes=[pltpu.VMEM((tm, d), BF16), pltpu.VMEM((tm, d), F32)],
        compiler_params=pltpu.CompilerParams(dimension_semantics=("arbitrary", "arbitrary"), vmem_limit_bytes=VMEM_LIMIT),
        name="mlp",
    )(x2, g_mlp, w_up, w_down)


def _pack_w_in(w_in, dm):
    sizes = (dm["ssd_w"], dm["conv_ch"], dm["ssd_heads"], dm["n_q"], dm["n_kv"], dm["n_kv"], dm["n_qi"],
             IDX_HEAD_DIM, IDX_HEADS)
    pts = [int(v) for v in np.cumsum(sizes)[:-1]]
    z, xbc, dt, q, k, v, qi, ki, wi = jnp.split(w_in, pts, axis=2)
    pad = jnp.zeros(w_in.shape[:2] + (LANES - IDX_HEADS - dm["ssd_heads"],), w_in.dtype)
    return jnp.concatenate([xbc, ki, ki, wi, dt, pad, z, q, k, v, qi], axis=2).astype(BF16)


def _expand_matrix(dm):
    e = np.zeros((LANES, dm["ssd_w"]), np.float32)
    for h in range(dm["ssd_heads"]):
        e[DT_LO + h, h * SSD_HEAD_DIM:(h + 1) * SSD_HEAD_DIM] = 1.0
    return jnp.asarray(np.concatenate([e, e, e], axis=0)).astype(BF16)


def _lane_pad(v, lo):
    return jnp.zeros((1, LANES), F32).at[0, lo:lo + v.shape[0]].set(v)


def kernel(x, g_mix, w_in, conv_w, conv_b, dt_bias, a_log, d_skip, g_ssd, g_q, g_k, g_kidx, w_out, g_mlp, w_up, w_down):
    batch, seq, d = x.shape
    depth = w_in.shape[0]
    dm = _dims(d, seq)
    assert dm["ssd_heads"] + IDX_HEADS <= LANES and (dm["ssd_heads"] // SSD_GROUPS) % 2 == 0
    assert seq % SSD_CHUNK == 0 and dm["att_heads"] % ATT_KV_HEADS == 0
    qb = 256
    tm = min(512, batch * seq)
    tf = min(1024, w_up.shape[2])
    assert seq % qb == 0 and (batch * seq) % tm == 0 and w_up.shape[2] % tf == 0
    e_mat = _expand_matrix(dm)
    w_all = _pack_w_in(w_in, dm)
    w_out, w_up, w_down = (w.astype(BF16) for w in (w_out, w_up, w_down))

    x2 = x.reshape(batch * seq, d)
    for i in range(depth):
        y_ssd, q, k, vt, qi, kia, kib, sm = _in_proj(
            x2, g_mix[i][None, :], w_all, g_q[i][None, :], g_k[i][None, :],
            jnp.concatenate([g_kidx[i], g_kidx[i]])[None, :], _lane_pad(dt_bias[i], DT_LO),
            conv_w[i], conv_b[i][None, :], _lane_pad(a_log[i], DT_LO),
            jnp.repeat(d_skip[i], SSD_HEAD_DIM)[None, :], g_ssd[i][None, :], e_mat,
            layer=i, dm=dm, tm=qb, seq=seq)
        y_att = _dsa(q, k, vt, qi, kia, kib, sm, dm=dm, batch=batch, seq=seq, qb=qb)
        x2 = _out_proj(x2, y_ssd, y_att, w_out, layer=i, tm=tm)
        x2 = _mlp(x2, g_mlp[i][None, :], w_up, w_down, layer=i, tm=tm, tf=tf)
    return x2.reshape(batch, seq, d)
```

```python
import functools

import numpy as np
import jax
import jax.numpy as jnp
from jax import lax
from jax.experimental import pallas as pl
from jax.experimental.pallas import tpu as pltpu

EPS = 1e-6
SSD_HEAD_DIM = 64
SSD_GROUPS = 2
SSD_STATE = 128
SSD_CONV = 4
SSD_CHUNK = 128
ATT_HEAD_DIM = 128
ATT_KV_HEADS = 4
IDX_HEADS = 16
IDX_HEAD_DIM = 64
IDX_TOPK_MAX = 256

LANES = 128
SUBLANES = 8
VMEM_LIMIT = 56 * 1024 * 1024

WI_LO, DT_LO = 0, 16
NEG_BIG = -1e30
INT_MIN = -(2 ** 31)
KEY_NEG_INF = 0x807FFFFF - 2 ** 32
REFINE_STEPS = 30
LOG2_E = 1.4426950408889634
SUM_ROWS = 16

F32 = jnp.float32
BF16 = jnp.bfloat16
HIGHEST = lax.Precision.HIGHEST


def _dims(d_model, seq):
    ssd_w = d_model // 2
    att_w = d_model - ssd_w
    n_bc = SSD_GROUPS * SSD_STATE
    return dict(
        ssd_w=ssd_w, ssd_heads=ssd_w // SSD_HEAD_DIM, n_bc=n_bc, conv_ch=ssd_w + 2 * n_bc,
        n_q=att_w, att_heads=att_w // ATT_HEAD_DIM, n_kv=ATT_KV_HEADS * ATT_HEAD_DIM,
        n_qi=IDX_HEADS * IDX_HEAD_DIM, topk=min(IDX_TOPK_MAX, seq // 4))


def _nt_dot(a, b):
    return lax.dot_general(a, b, (((1,), (1,)), ((), ())), preferred_element_type=F32)


def _silu(v):
    return v * (1.0 / (1.0 + jnp.exp(-v)))


def _softplus(v):
    return jnp.maximum(v, 0.0) + jnp.log1p(jnp.exp(-jnp.abs(v)))


def _split3(a):
    hi = a.astype(BF16)
    rest = a - hi.astype(F32)
    mid = rest.astype(BF16)
    lo = (rest - mid.astype(F32)).astype(BF16)
    return jnp.concatenate([hi, mid, lo], axis=1)


def _in_proj_kernel(x_ref, gmix_ref, w_ref, gq_ref, gk_ref, gki_ref, dtb_ref, cw_ref, cb_ref,
                    alog_ref, dskip_ref, gssd_ref, e_ref,
                    y_ref, q_ref, k_ref, vt_ref, qi_ref, kia_ref, kib_ref, sm_ref, ubuf_ref, state_ref,
                    *, dm, tiles_per_seq):
    tm = x_ref.shape[0]

    @pl.when(pl.program_id(0) % tiles_per_seq == 0)
    def _():
        ubuf_ref[0:SUBLANES, :] = jnp.zeros((SUBLANES, ubuf_ref.shape[1]), F32)
        state_ref[...] = jnp.zeros_like(state_ref)

    x = x_ref[...]
    ms = jnp.mean(x * x, axis=-1, keepdims=True)
    xn = (x * lax.rsqrt(ms + EPS) * gmix_ref[...]).astype(BF16)

    off = [0]

    def seg(width):
        lo = off[0]
        off[0] = lo + width
        return jnp.dot(xn, w_ref[:, lo:lo + width], preferred_element_type=F32)

    u = seg(dm["conv_ch"])
    ubuf_ref[SUBLANES:SUBLANES + tm, :] = u
    lane = lax.broadcasted_iota(jnp.int32, (1, LANES), 1)
    tail = seg(2 * LANES)
    gz = _silu(seg(dm["ssd_w"]))

    def head_norm(h, gain, n_heads, out_ref):
        for i in range(n_heads):
            hh = h[:, i * ATT_HEAD_DIM:(i + 1) * ATT_HEAD_DIM]
            m2 = jnp.mean(hh * hh, axis=-1, keepdims=True)
            out_ref[:, i * ATT_HEAD_DIM:(i + 1) * ATT_HEAD_DIM] = (hh * lax.rsqrt(m2 + EPS) * gain).astype(out_ref.dtype)

    head_norm(seg(dm["n_q"]), gq_ref[...] * (ATT_HEAD_DIM ** -0.5 * LOG2_E), dm["att_heads"], q_ref)
    head_norm(seg(dm["n_kv"]), gk_ref[...], ATT_KV_HEADS, k_ref)
    vt_ref[0] = seg(dm["n_kv"]).T.astype(BF16)
    qi_ref[...] = (seg(dm["n_qi"]) * (IDX_HEAD_DIM ** -0.5)).astype(BF16)

    kk = tail[:, :LANES]
    m2 = jnp.mean(kk * kk, axis=-1, keepdims=True)
    kin = kk * lax.rsqrt(m2 + EPS) * gki_ref[...]
    kia_ref[...] = jnp.where(lane < IDX_HEAD_DIM, kin, 0.0).astype(BF16)
    kib_ref[...] = jnp.where(lane >= IDX_HEAD_DIM, kin, 0.0).astype(BF16)

    sm = tail[:, LANES:]
    wi = sm * (IDX_HEADS ** -0.5)
    dt = _softplus(sm + dtb_ref[...])
    sm_val = jnp.where(lane < DT_LO, wi, jnp.where(lane < DT_LO + dm["ssd_heads"], dt, 0.0))
    sm_ref[...] = sm_val

    acc = cb_ref[...] + cw_ref[SSD_CONV - 1:SSD_CONV, :] * u
    for s in range(1, SSD_CONV):
        acc = acc + cw_ref[SSD_CONV - 1 - s:SSD_CONV - s, :] * ubuf_ref[SUBLANES - s:SUBLANES - s + tm, :]
    ubuf_ref[0:SUBLANES, :] = ubuf_ref[tm:tm + SUBLANES, :]
    xc = _silu(acc)

    q = SSD_CHUNK
    is_dt = (lane >= DT_LO) & (lane < DT_LO + dm["ssd_heads"])
    a_full = jnp.where(is_dt, -jnp.exp(alog_ref[...]), 0.0)
    causal = lax.broadcasted_iota(jnp.int32, (q, q), 0) >= lax.broadcasted_iota(jnp.int32, (q, q), 1)
    tri = causal.astype(F32)
    for r0 in range(0, tm, q):
        _ssd_chunk(xc[r0:r0 + q, :], gz[r0:r0 + q, :], sm_val[r0:r0 + q, :], a_full, is_dt, lane, causal, tri,
                   dskip_ref, gssd_ref, e_ref, y_ref.at[r0:r0 + q, :], state_ref, dm=dm)


def _in_proj(x2, gmix, w_all, gq, gk, gki2, dtb, conv_w, conv_b, alog_pad, dskip_x, g_ssd, e_mat,
             *, layer, dm, tm, seq):
    t, d = x2.shape
    assert seq % tm == 0 and t % tm == 0 and tm % SSD_CHUNK == 0
    nw = w_all.shape[2]
    n_kv = dm["n_kv"]
    row = lambda width: pl.BlockSpec((tm, width), lambda i: (i, 0))
    const = lambda shape: pl.BlockSpec(shape, lambda i: (0,) * len(shape))
    out_shape = (
        jax.ShapeDtypeStruct((t, dm["ssd_w"]), BF16),
        jax.ShapeDtypeStruct((t, dm["n_q"]), BF16),
        jax.ShapeDtypeStruct((t, n_kv), BF16),
        jax.ShapeDtypeStruct((t // tm, n_kv, tm), BF16),
        jax.ShapeDtypeStruct((t, dm["n_qi"]), BF16),
        jax.ShapeDtypeStruct((t, LANES), BF16),
        jax.ShapeDtypeStruct((t, LANES), BF16),
        jax.ShapeDtypeStruct((t, LANES), F32),
    )
    out_specs = (
        row(dm["ssd_w"]), row(dm["n_q"]), row(n_kv),
        pl.BlockSpec((1, n_kv, tm), lambda i: (i, 0, 0)),
        row(dm["n_qi"]), row(LANES), row(LANES), row(LANES),
    )
    return pl.pallas_call(
        functools.partial(_in_proj_kernel, dm=dm, tiles_per_seq=seq // tm),
        grid=(t // tm,),
        in_specs=[
            row(d), const((1, d)),
            pl.BlockSpec((None, d, nw), lambda i: (layer, 0, 0), pipeline_mode=pl.Buffered(1)),
            const((1, ATT_HEAD_DIM)), const((1, ATT_HEAD_DIM)), const((1, LANES)), const((1, LANES)),
            const((SSD_CONV, dm["conv_ch"])), const((1, dm["conv_ch"])),
            const((1, LANES)), const((1, dm["ssd_w"])), const((1, dm["ssd_w"])), const((3 * LANES, dm["ssd_w"])),
        ],
        out_specs=out_specs,
        out_shape=out_shape,
        scratch_shapes=[pltpu.VMEM((tm + SUBLANES, dm["conv_ch"]), F32), pltpu.VMEM((SSD_STATE, dm["ssd_w"]), F32)],
        compiler_params=pltpu.CompilerParams(dimension_semantics=("arbitrary",), vmem_limit_bytes=VMEM_LIMIT),
        name="in_proj",
    )(x2, gmix, w_all, gq, gk, gki2, dtb, conv_w, conv_b, alog_pad, dskip_x, g_ssd, e_mat)


def _ssd_chunk(xc, gz, sm, a_full, is_dt, lane, causal, tri, dskip_ref, gssd_ref, e_ref, y_ref, state_ref, *, dm):
    q = SSD_CHUNK
    ssd_w = dm["ssd_w"]
    n_bc = dm["n_bc"]
    gw = ssd_w // SSD_GROUPS
    hpg = dm["ssd_heads"] // SSD_GROUPS
    xs = xc[:, :ssd_w]
    bm = xc[:, ssd_w:ssd_w + n_bc]
    cm = xc[:, ssd_w + n_bc:ssd_w + 2 * n_bc]
    dt_full = jnp.where(is_dt, sm, 0.0)
    adt = dt_full * a_full
    a_cum = jnp.dot(tri, adt, precision=HIGHEST, preferred_element_type=F32)
    a_cum_t = a_cum.T
    both = jnp.dot(_split3(jnp.concatenate([a_cum, dt_full], axis=0)), e_ref[...], preferred_element_type=F32)
    acum_x = both[:q]
    dt_x = both[q:]
    alast_x = acum_x[q - 1:q, :]
    decay_in = jnp.exp(acum_x)
    decay_out = jnp.exp(alast_x - acum_x)
    chunk_decay = jnp.exp(alast_x)

    xdt = xs * dt_x
    xdt_b = xdt.astype(BF16)
    xw_b = (xdt * decay_out).astype(BF16)

    y_diag, y_off = [], []
    for g in range(SSD_GROUPS):
        cm_g = cm[:, g * SSD_STATE:(g + 1) * SSD_STATE]
        bm_g = bm[:, g * SSD_STATE:(g + 1) * SSD_STATE]
        cm_b = cm_g.astype(BF16)
        cb = _nt_dot(cm_b, bm_g.astype(BF16))
        s_prev = state_ref[:, g * gw:(g + 1) * gw]
        y_off.append(jnp.dot(cm_b, s_prev.astype(BF16), preferred_element_type=F32))
        state_ref[:, g * gw:(g + 1) * gw] = (
            s_prev * chunk_decay[:, g * gw:(g + 1) * gw]
            + jnp.dot(bm_g.T.astype(BF16), xw_b[:, g * gw:(g + 1) * gw], preferred_element_type=F32))
        for j in range(hpg // 2):
            h0 = g * hpg + 2 * j
            xp = xdt_b[:, h0 * SSD_HEAD_DIM:(h0 + 2) * SSD_HEAD_DIM]
            pair = None
            for r in range(2):
                c = DT_LO + h0 + r
                seg = a_cum[:, c:c + 1] - a_cum_t[c:c + 1, :]
                decay = jnp.exp(jnp.where(causal, seg, -jnp.inf))
                m = (cb * decay).astype(BF16)
                keep = (lane < SSD_HEAD_DIM) if r == 0 else (lane >= SSD_HEAD_DIM)
                part = jnp.dot(m, jnp.where(keep, xp, jnp.zeros_like(xp)), preferred_element_type=F32)
                pair = part if pair is None else pair + part
            y_diag.append(pair)

    y = jnp.concatenate(y_diag, axis=1) + decay_in * jnp.concatenate(y_off, axis=1) + dskip_ref[...] * xs
    y = y * gz
    outs = []
    for g in range(SSD_GROUPS):
        yg = y[:, g * gw:(g + 1) * gw]
        outs.append(yg * lax.rsqrt(jnp.mean(yg * yg, axis=-1, keepdims=True) + EPS))
    y_ref[...] = (jnp.concatenate(outs, axis=1) * gssd_ref[...]).astype(y_ref.dtype)


def _key_to_f32(ks):
    return lax.bitcast_convert_type(ks ^ ((ks >> 31) & jnp.int32(0x7FFFFFFF)), F32)


COUNT_ACCS = 4


def _count_rows(m, cnt):
    cnt = list(cnt)
    for j, r in enumerate(range(0, m.shape[0], SUBLANES)):
        a = j % COUNT_ACCS
        cnt[a] = jnp.where(m[r:r + SUBLANES, :], cnt[a] + 1, cnt[a])
    return tuple(cnt)


def _dsa_kernel(q_ref, k_ref, vt_ref, qi_ref, kia_ref, kib_ref, sm_ref, o_ref,
                score_ref, bias_ref, thr_ref, mstar_ref, m_ref, acc_ref, logit_ref, cmax_ref, *, dm, seq, qb):
    kc = qb
    topk = dm["topk"]
    n_heads = dm["att_heads"]
    rep = n_heads // ATT_KV_HEADS
    i = pl.program_id(1)
    nkc = i + 1
    q_pos = i * qb + lax.broadcasted_iota(jnp.int32, (1, qb), 1)

    def key_pos(c):
        return c * kc + lax.broadcasted_iota(jnp.int32, (kc, 1), 0)

    def rows(c):
        return pl.ds(pl.multiple_of(c * kc, kc), kc)

    wi_t = sm_ref[...].T
    qi = qi_ref[...]

    def score_chunk(c):
        ka = kia_ref[rows(c), :]
        kb = kib_ref[rows(c), :]
        acc = jnp.zeros((kc, qb), F32)
        for j in range(IDX_HEADS // 2):
            qp = qi[:, j * LANES:(j + 1) * LANES]
            w0 = wi_t[WI_LO + 2 * j:WI_LO + 2 * j + 1, :]
            w1 = wi_t[WI_LO + 2 * j + 1:WI_LO + 2 * j + 2, :]
            acc = acc + w0 * jnp.maximum(_nt_dot(ka, qp), 0.0) + w1 * jnp.maximum(_nt_dot(kb, qp), 0.0)
        score_ref[rows(c), :] = jnp.where(key_pos(c) <= q_pos, acc, -jnp.inf)

    def pairwise(one_chunk):
        def two(j, carry):
            one_chunk(2 * j)
            one_chunk(2 * j + 1)
            return carry

        lax.fori_loop(0, nkc // 2, two, 0)

        @pl.when(nkc % 2 == 1)
        def _():
            one_chunk(nkc - 1)

    pairwise(score_chunk)

    def count(pred):
        def body(c, cnt):
            return _count_rows(pred(score_ref[rows(c), :], c), cnt)
        cnt = lax.fori_loop(0, nkc, body, (jnp.zeros((SUBLANES, qb), jnp.int32),) * COUNT_ACCS)
        return jnp.sum(functools.reduce(lambda a, b: a + b, cnt), axis=0, keepdims=True)

    def bit_step(p, carry):
        res, n_res = carry
        cand = res | (jnp.int32(1) << (31 - p))
        cand_s = cand ^ jnp.int32(INT_MIN)
        cand_f = _key_to_f32(cand_s)
        n_cand = count(lambda sc, c: sc >= cand_f)
        ok = (n_cand >= topk) | (cand_s <= jnp.int32(KEY_NEG_INF))
        return jnp.where(ok, cand, res), jnp.where(ok, n_cand, n_res)

    zeros_q = jnp.zeros((1, qb), jnp.int32)
    res, n_ge = lax.fori_loop(0, 32, bit_step, (zeros_q, zeros_q))
    thr_s = res ^ jnp.int32(INT_MIN)
    thr0 = _key_to_f32(thr_s)
    thr_ref[...] = thr0
    mstar_ref[...] = jnp.full((1, qb), seq, jnp.int32)

    @pl.when(jnp.max(n_ge) > topk)
    def _():
        def refine(_, lh):
            lo, hi = lh
            mid = lo + 0.5 * (hi - lo)
            ok = count(lambda sc, c: sc >= mid) >= topk
            return jnp.where(ok, mid, lo), jnp.where(ok, hi, mid)

        thr, _ = lax.fori_loop(0, REFINE_STEPS, refine, (thr0, _key_to_f32(thr_s + 1)))
        thr_ref[...] = thr
        need = topk - count(lambda sc, c: sc > thr)
        nbits = max(1, int(np.ceil(np.log2(seq))))

        def idx_step(p, res):
            cand = res | (jnp.int32(1) << (nbits - 1 - p))
            n_eq_before = count(lambda sc, c: (sc == thr) & (key_pos(c) < cand))
            return jnp.where(n_eq_before < need, cand, res)

        mstar_ref[...] = lax.fori_loop(0, nbits, idx_step, jnp.zeros((1, qb), jnp.int32))

    thr = thr_ref[...]
    mstar = mstar_ref[...]

    m_ref[...] = jnp.full(m_ref.shape, NEG_BIG, F32)
    acc_ref[...] = jnp.zeros(acc_ref.shape, F32)

    ones_rows = jnp.ones((SUM_ROWS, kc), BF16)

    def logits_phase(c):
        slot = (c % 2) * n_heads
        sc = score_ref[rows(c), :]
        kp = key_pos(c)
        sel = ((sc > thr) | ((sc == thr) & (kp <= mstar))) & (kp <= q_pos)
        bias_ref[rows(c), :] = jnp.where(sel, 0.0, NEG_BIG)
        for h in range(n_heads):
            g = h // rep
            kch = k_ref[rows(c), g * ATT_HEAD_DIM:(g + 1) * ATT_HEAD_DIM]
            logit = _nt_dot(kch, q_ref[:, h * ATT_HEAD_DIM:(h + 1) * ATT_HEAD_DIM]) + bias_ref[rows(c), :]
            logit_ref[slot + h] = logit
            cmax_ref[slot + h, 0:1, :] = jnp.max(logit, axis=0, keepdims=True)

    def softmax_phase(c):
        slot = (c % 2) * n_heads
        for h in range(n_heads):
            g = h // rep
            vt = jnp.concatenate([vt_ref[c, g * ATT_HEAD_DIM:(g + 1) * ATT_HEAD_DIM, :], ones_rows], axis=0)
            m_old = m_ref[h:h + 1, :]
            m_new = jnp.maximum(m_old, cmax_ref[slot + h, 0:1, :])
            p = jnp.exp2(logit_ref[slot + h] - m_new)
            m_ref[h:h + 1, :] = m_new
            acc_ref[h] = (acc_ref[h] * jnp.exp2(m_old - m_new)
                          + jnp.dot(vt, p.astype(BF16), preferred_element_type=F32))

    def att_chunk(c):
        logits_phase(c)
        softmax_phase(c)

    pairwise(att_chunk)
    for h in range(n_heads):
        acc = acc_ref[h]
        o = acc[:ATT_HEAD_DIM, :] / acc[ATT_HEAD_DIM:ATT_HEAD_DIM + 1, :]
        o_ref[:, h * ATT_HEAD_DIM:(h + 1) * ATT_HEAD_DIM] = o.T.astype(o_ref.dtype)


def _dsa(q, k, vt, qi, kia, kib, sm, *, dm, batch, seq, qb):
    t = q.shape[0]
    nqb = seq // qb
    n_kv = dm["n_kv"]
    assert qb >= dm["topk"] and seq % qb == 0
    qrow = lambda width: pl.BlockSpec((qb, width), lambda b, i: (b * nqb + i, 0))
    brow = lambda width: pl.BlockSpec((seq, width), lambda b, i: (b, 0))
    return pl.pallas_call(
        functools.partial(_dsa_kernel, dm=dm, seq=seq, qb=qb),
        grid=(batch, nqb),
        in_specs=[qrow(dm["n_q"]), brow(n_kv),
                  pl.BlockSpec((nqb, n_kv, qb), lambda b, i: (b, 0, 0)),
                  qrow(dm["n_qi"]), brow(LANES), brow(LANES), qrow(LANES)],
        out_specs=qrow(dm["n_q"]),
        out_shape=jax.ShapeDtypeStruct((t, dm["n_q"]), BF16),
        scratch_shapes=[pltpu.VMEM((seq, qb), F32), pltpu.VMEM((seq, qb), F32),
                        pltpu.VMEM((1, qb), F32), pltpu.VMEM((1, qb), jnp.int32),
                        pltpu.VMEM((dm["att_heads"], qb), F32),
                        pltpu.VMEM((dm["att_heads"], ATT_HEAD_DIM + SUM_ROWS, qb), F32),
                        pltpu.VMEM((2 * dm["att_heads"], qb, qb), F32),
                        pltpu.VMEM((2 * dm["att_heads"], SUBLANES, qb), F32)],
        compiler_params=pltpu.CompilerParams(dimension_semantics=("arbitrary", "arbitrary"), vmem_limit_bytes=VMEM_LIMIT),
        name="dsa",
    )(q, k, vt, qi, kia, kib, sm)


def _out_proj_kernel(x_ref, ya_ref, yb_ref, w_ref, o_ref):
    na = ya_ref.shape[1]
    o_ref[...] = (x_ref[...]
                  + jnp.dot(ya_ref[...], w_ref[:na, :], preferred_element_type=F32)
                  + jnp.dot(yb_ref[...], w_ref[na:, :], preferred_element_type=F32))


def _out_proj(x2, y_ssd, y_att, w_out, *, layer, tm):
    t, d = x2.shape
    row = lambda width: pl.BlockSpec((tm, width), lambda i: (i, 0))
    return pl.pallas_call(
        _out_proj_kernel,
        grid=(t // tm,),
        in_specs=[row(d), row(y_ssd.shape[1]), row(y_att.shape[1]),
                  pl.BlockSpec((None,) + w_out.shape[1:], lambda i: (layer, 0, 0), pipeline_mode=pl.Buffered(1))],
        out_specs=row(d),
        out_shape=jax.ShapeDtypeStruct((t, d), F32),
        compiler_params=pltpu.CompilerParams(dimension_semantics=("arbitrary",), vmem_limit_bytes=VMEM_LIMIT),
        name="out_proj",
    )(x2, y_ssd, y_att, w_out)


def _mlp_kernel(x_ref, g_ref, wu_ref, wd_ref, o_ref, xn_ref, acc_ref):
    f = pl.program_id(1)

    @pl.when(f == 0)
    def _():
        x = x_ref[...]
        ms = jnp.mean(x * x, axis=-1, keepdims=True)
        xn_ref[...] = (x * lax.rsqrt(ms + EPS) * g_ref[...]).astype(BF16)
        acc_ref[...] = jnp.zeros_like(acc_ref)

    u = jnp.maximum(jnp.dot(xn_ref[...], wu_ref[...], preferred_element_type=F32), 0.0)
    acc_ref[...] += jnp.dot((u * u).astype(BF16), wd_ref[...], preferred_element_type=F32)

    @pl.when(f == pl.num_programs(1) - 1)
    def _():
        o_ref[...] = x_ref[...] + acc_ref[...]


def _mlp(x2, g_mlp, w_up, w_down, *, layer, tm, tf):
    t, d = x2.shape
    d_ff = w_up.shape[2]
    return pl.pallas_call(
        _mlp_kernel,
        grid=(t // tm, d_ff // tf),
        in_specs=[pl.BlockSpec((tm, d), lambda i, f: (i, 0)),
                  pl.BlockSpec((1, d), lambda i, f: (0, 0)),
                  pl.BlockSpec((None, d, tf), lambda i, f: (layer, 0, f)),
                  pl.BlockSpec((None, tf, d), lambda i, f: (layer, f, 0))],
        out_specs=pl.BlockSpec((tm, d), lambda i, f: (i, 0)),
        out_shape=jax.ShapeDtypeStruct((t, d), F32),
        scratch_shapes=[pltpu.VMEM((tm, d), BF16), pltpu.VMEM((tm, d), F32)],
        compiler_params=pltpu.CompilerParams(dimension_semantics=("arbitrary", "arbitrary"), vmem_limit_bytes=VMEM_LIMIT),
        name="mlp",
    )(x2, g_mlp, w_up, w_down)


def _pack_w_kernel(w_ref, tail_ref, o_ref, *, dm):
    z_w, cc = dm["ssd_w"], dm["conv_ch"]
    rest_lo = z_w + cc + dm["ssd_heads"]
    rest_w = dm["n_q"] + 2 * dm["n_kv"] + dm["n_qi"]
    base = rest_lo // LANES * LANES
    o_ref[:, 0:cc] = w_ref[:, z_w:z_w + cc].astype(BF16)
    o_ref[:, cc:cc + 2 * LANES] = tail_ref[...].astype(BF16)
    o_ref[:, cc + 2 * LANES:cc + 2 * LANES + z_w] = w_ref[:, 0:z_w].astype(BF16)
    rest = w_ref[:, base:][:, rest_lo - base:rest_lo - base + rest_w]
    o_ref[:, cc + 2 * LANES + z_w:] = rest.astype(BF16)


def _pack_w_in(w_in, dm):
    depth, d, n_in = w_in.shape
    sizes = (dm["ssd_w"], dm["conv_ch"], dm["ssd_heads"], dm["n_q"], dm["n_kv"], dm["n_kv"], dm["n_qi"],
             IDX_HEAD_DIM, IDX_HEADS)
    assert sum(sizes) == n_in
    lo = int(np.sum(sizes[:2]))
    dt = w_in[:, :, lo:lo + dm["ssd_heads"]]
    ki = w_in[:, :, n_in - IDX_HEADS - IDX_HEAD_DIM:n_in - IDX_HEADS]
    wi = w_in[:, :, n_in - IDX_HEADS:]
    pad = jnp.zeros((depth, d, LANES - IDX_HEADS - dm["ssd_heads"]), w_in.dtype)
    tail = jnp.concatenate([ki, ki, wi, dt, pad], axis=2)
    nw = n_in - dm["ssd_heads"] - IDX_HEAD_DIM - IDX_HEADS + 2 * LANES
    tr = min(256, d)
    assert d % tr == 0
    return pl.pallas_call(
        functools.partial(_pack_w_kernel, dm=dm),
        grid=(depth, d // tr),
        in_specs=[pl.BlockSpec((None, tr, n_in), lambda l, r: (l, r, 0)),
                  pl.BlockSpec((None, tr, 2 * LANES), lambda l, r: (l, r, 0))],
        out_specs=pl.BlockSpec((None, tr, nw), lambda l, r: (l, r, 0)),
        out_shape=jax.ShapeDtypeStruct((depth, d, nw), BF16),
        compiler_params=pltpu.CompilerParams(dimension_semantics=("arbitrary", "arbitrary"), vmem_limit_bytes=VMEM_LIMIT),
        name="pack_w_in",
    )(w_in, tail)


def _expand_matrix(dm):
    e = np.zeros((LANES, dm["ssd_w"]), np.float32)
    for h in range(dm["ssd_heads"]):
        e[DT_LO + h, h * SSD_HEAD_DIM:(h + 1) * SSD_HEAD_DIM] = 1.0
    return jnp.asarray(np.concatenate([e, e, e], axis=0)).astype(BF16)


def _lane_pad(v, lo):
    return jnp.zeros((1, LANES), F32).at[0, lo:lo + v.shape[0]].set(v)


def kernel(x, g_mix, w_in, conv_w, conv_b, dt_bias, a_log, d_skip, g_ssd, g_q, g_k, g_kidx, w_out, g_mlp, w_up, w_down):
    batch, seq, d = x.shape
    depth = w_in.shape[0]
    dm = _dims(d, seq)
    assert dm["ssd_heads"] + IDX_HEADS <= LANES and (dm["ssd_heads"] // SSD_GROUPS) % 2 == 0
    assert seq % SSD_CHUNK == 0 and dm["att_heads"] % ATT_KV_HEADS == 0
    qb = 256
    tm = min(512, batch * seq)
    tf = min(1024, w_up.shape[2])
    assert seq % qb == 0 and (batch * seq) % tm == 0 and w_up.shape[2] % tf == 0
    e_mat = _expand_matrix(dm)
    w_all = _pack_w_in(w_in, dm)
    w_out, w_up, w_down = (w.astype(BF16) for w in (w_out, w_up, w_down))

    x2 = x.reshape(batch * seq, d)
    for i in range(depth):
        y_ssd, q, k, vt, qi, kia, kib, sm = _in_proj(
            x2, g_mix[i][None, :], w_all, g_q[i][None, :], g_k[i][None, :],
            jnp.concatenate([g_kidx[i], g_kidx[i]])[None, :], _lane_pad(dt_bias[i], DT_LO),
            conv_w[i], conv_b[i][None, :], _lane_pad(a_log[i], DT_LO),
            jnp.repeat(d_skip[i], SSD_HEAD_DIM)[None, :], g_ssd[i][None, :], e_mat,
            layer=i, dm=dm, tm=qb, seq=seq)
        y_att = _dsa(q, k, vt, qi, kia, kib, sm, dm=dm, batch=batch, seq=seq, qb=qb)
        x2 = _out_proj(x2, y_ssd, y_att, w_out, layer=i, tm=tm)
        x2 = _mlp(x2, g_mlp[i][None, :], w_up, w_down, layer=i, tm=tm, tf=tf)
    return x2.reshape(batch, seq, d)
```

```python
import functools

import numpy as np
import jax
import jax.numpy as jnp
from jax import lax
from jax.experimental import pallas as pl
from jax.experimental.pallas import tpu as pltpu

EPS = 1e-6
SSD_HEAD_DIM = 64
SSD_GROUPS = 2
SSD_STATE = 128
SSD_CONV = 4
SSD_CHUNK = 128
ATT_HEAD_DIM = 128
ATT_KV_HEADS = 4
IDX_HEADS = 16
IDX_HEAD_DIM = 64
IDX_TOPK_MAX = 256

LANES = 128
SUBLANES = 8
V7X_VMEM_BYTES = 64 * 1024 * 1024
VMEM_LIMIT = V7X_VMEM_BYTES - 8 * 1024 * 1024

WI_LO, DT_LO = 0, 16
NEG_BIG = -1e30
INT_MIN = -(2 ** 31)
KEY_NEG_INF = 0x807FFFFF - 2 ** 32
REFINE_STEPS = 30
LOG2_E = 1.4426950408889634
SUM_ROWS = 16

F32 = jnp.float32
BF16 = jnp.bfloat16
HIGHEST = lax.Precision.HIGHEST


def _dims(d_model, seq):
    ssd_w = d_model // 2
    att_w = d_model - ssd_w
    n_bc = SSD_GROUPS * SSD_STATE
    return dict(
        ssd_w=ssd_w, ssd_heads=ssd_w // SSD_HEAD_DIM, n_bc=n_bc, conv_ch=ssd_w + 2 * n_bc,
        n_q=att_w, att_heads=att_w // ATT_HEAD_DIM, n_kv=ATT_KV_HEADS * ATT_HEAD_DIM,
        n_qi=IDX_HEADS * IDX_HEAD_DIM, topk=min(IDX_TOPK_MAX, seq // 4))


def _nt_dot(a, b):
    return lax.dot_general(a, b, (((1,), (1,)), ((), ())), preferred_element_type=F32)


def _silu(v):
    return v * (1.0 / (1.0 + jnp.exp(-v)))


def _softplus(v):
    return jnp.maximum(v, 0.0) + jnp.log1p(jnp.exp(-jnp.abs(v)))


def _split3(a):
    hi = a.astype(BF16)
    rest = a - hi.astype(F32)
    mid = rest.astype(BF16)
    lo = (rest - mid.astype(F32)).astype(BF16)
    return jnp.concatenate([hi, mid, lo], axis=1)


def _in_proj_kernel(x_ref, gmix_ref, w_ref, gq_ref, gk_ref, gki_ref, dtb_ref, cw_ref, cb_ref,
                    alog_ref, dskip_ref, gssd_ref, e_ref,
                    y_ref, q_ref, k_ref, vt_ref, qi_ref, kia_ref, kib_ref, sm_ref, ubuf_ref, state_ref,
                    *, dm, tiles_per_seq):
    tm = x_ref.shape[0]

    @pl.when(pl.program_id(0) % tiles_per_seq == 0)
    def _():
        ubuf_ref[0:SUBLANES, :] = jnp.zeros((SUBLANES, ubuf_ref.shape[1]), F32)
        state_ref[...] = jnp.zeros_like(state_ref)

    x = x_ref[...]
    ms = jnp.mean(x * x, axis=-1, keepdims=True)
    xn = (x * lax.rsqrt(ms + EPS) * gmix_ref[...]).astype(BF16)

    off = [0]

    def seg(width):
        lo = off[0]
        off[0] = lo + width
        return jnp.dot(xn, w_ref[:, lo:lo + width], preferred_element_type=F32)

    u = seg(dm["conv_ch"])
    ubuf_ref[SUBLANES:SUBLANES + tm, :] = u
    lane = lax.broadcasted_iota(jnp.int32, (1, LANES), 1)
    tail = seg(2 * LANES)
    gz = _silu(seg(dm["ssd_w"]))

    def head_norm(h, gain, n_heads, out_ref):
        for i in range(n_heads):
            hh = h[:, i * ATT_HEAD_DIM:(i + 1) * ATT_HEAD_DIM]
            m2 = jnp.mean(hh * hh, axis=-1, keepdims=True)
            out_ref[:, i * ATT_HEAD_DIM:(i + 1) * ATT_HEAD_DIM] = (hh * lax.rsqrt(m2 + EPS) * gain).astype(out_ref.dtype)

    head_norm(seg(dm["n_q"]), gq_ref[...] * (ATT_HEAD_DIM ** -0.5 * LOG2_E), dm["att_heads"], q_ref)
    head_norm(seg(dm["n_kv"]), gk_ref[...], ATT_KV_HEADS, k_ref)
    vt_ref[0] = seg(dm["n_kv"]).T.astype(BF16)
    qi_ref[...] = (seg(dm["n_qi"]) * (IDX_HEAD_DIM ** -0.5)).astype(BF16)

    kk = tail[:, :LANES]
    m2 = jnp.mean(kk * kk, axis=-1, keepdims=True)
    kin = kk * lax.rsqrt(m2 + EPS) * gki_ref[...]
    kia_ref[...] = jnp.where(lane < IDX_HEAD_DIM, kin, 0.0).astype(BF16)
    kib_ref[...] = jnp.where(lane >= IDX_HEAD_DIM, kin, 0.0).astype(BF16)

    sm = tail[:, LANES:]
    wi = sm * (IDX_HEADS ** -0.5)
    dt = _softplus(sm + dtb_ref[...])
    sm_val = jnp.where(lane < DT_LO, wi, jnp.where(lane < DT_LO + dm["ssd_heads"], dt, 0.0))
    sm_ref[...] = sm_val

    acc = cb_ref[...] + cw_ref[SSD_CONV - 1:SSD_CONV, :] * u
    for s in range(1, SSD_CONV):
        acc = acc + cw_ref[SSD_CONV - 1 - s:SSD_CONV - s, :] * ubuf_ref[SUBLANES - s:SUBLANES - s + tm, :]
    ubuf_ref[0:SUBLANES, :] = ubuf_ref[tm:tm + SUBLANES, :]
    xc = _silu(acc)

    q = SSD_CHUNK
    is_dt = (lane >= DT_LO) & (lane < DT_LO + dm["ssd_heads"])
    a_full = jnp.where(is_dt, -jnp.exp(alog_ref[...]), 0.0)
    causal = lax.broadcasted_iota(jnp.int32, (q, q), 0) >= lax.broadcasted_iota(jnp.int32, (q, q), 1)
    tri = causal.astype(F32)
    for r0 in range(0, tm, q):
        _ssd_chunk(xc[r0:r0 + q, :], gz[r0:r0 + q, :], sm_val[r0:r0 + q, :], a_full, is_dt, lane, causal, tri,
                   dskip_ref, gssd_ref, e_ref, y_ref.at[r0:r0 + q, :], state_ref, dm=dm)


def _in_proj(x2, gmix, w_all, gq, gk, gki2, dtb, conv_w, conv_b, alog_pad, dskip_x, g_ssd, e_mat,
             *, layer, dm, tm, seq):
    t, d = x2.shape
    assert seq % tm == 0 and t % tm == 0 and tm % SSD_CHUNK == 0
    nw = w_all.shape[2]
    n_kv = dm["n_kv"]
    row = lambda width: pl.BlockSpec((tm, width), lambda i: (i, 0))
    const = lambda shape: pl.BlockSpec(shape, lambda i: (0,) * len(shape))
    out_shape = (
        jax.ShapeDtypeStruct((t, dm["ssd_w"]), BF16),
        jax.ShapeDtypeStruct((t, dm["n_q"]), BF16),
        jax.ShapeDtypeStruct((t, n_kv), BF16),
        jax.ShapeDtypeStruct((t // tm, n_kv, tm), BF16),
        jax.ShapeDtypeStruct((t, dm["n_qi"]), BF16),
        jax.ShapeDtypeStruct((t, LANES), BF16),
        jax.ShapeDtypeStruct((t, LANES), BF16),
        jax.ShapeDtypeStruct((t, LANES), F32),
    )
    out_specs = (
        row(dm["ssd_w"]), row(dm["n_q"]), row(n_kv),
        pl.BlockSpec((1, n_kv, tm), lambda i: (i, 0, 0)),
        row(dm["n_qi"]), row(LANES), row(LANES), row(LANES),
    )
    return pl.pallas_call(
        functools.partial(_in_proj_kernel, dm=dm, tiles_per_seq=seq // tm),
        grid=(t // tm,),
        in_specs=[
            row(d), const((1, d)),
            pl.BlockSpec((None, d, nw), lambda i: (layer, 0, 0), pipeline_mode=pl.Buffered(1)),
            const((1, ATT_HEAD_DIM)), const((1, ATT_HEAD_DIM)), const((1, LANES)), const((1, LANES)),
            const((SSD_CONV, dm["conv_ch"])), const((1, dm["conv_ch"])),
            const((1, LANES)), const((1, dm["ssd_w"])), const((1, dm["ssd_w"])), const((3 * LANES, dm["ssd_w"])),
        ],
        out_specs=out_specs,
        out_shape=out_shape,
        scratch_shapes=[pltpu.VMEM((tm + SUBLANES, dm["conv_ch"]), F32), pltpu.VMEM((SSD_STATE, dm["ssd_w"]), F32)],
        compiler_params=pltpu.CompilerParams(dimension_semantics=("arbitrary",), vmem_limit_bytes=VMEM_LIMIT),
        name="in_proj",
    )(x2, gmix, w_all, gq, gk, gki2, dtb, conv_w, conv_b, alog_pad, dskip_x, g_ssd, e_mat)


def _ssd_chunk(xc, gz, sm, a_full, is_dt, lane, causal, tri, dskip_ref, gssd_ref, e_ref, y_ref, state_ref, *, dm):
    q = SSD_CHUNK
    ssd_w = dm["ssd_w"]
    n_bc = dm["n_bc"]
    gw = ssd_w // SSD_GROUPS
    hpg = dm["ssd_heads"] // SSD_GROUPS
    xs = xc[:, :ssd_w]
    bm = xc[:, ssd_w:ssd_w + n_bc]
    cm = xc[:, ssd_w + n_bc:ssd_w + 2 * n_bc]
    dt_full = jnp.where(is_dt, sm, 0.0)
    adt = dt_full * a_full
    a_cum = jnp.dot(tri, adt, precision=HIGHEST, preferred_element_type=F32)
    a_cum_t = a_cum.T
    both = jnp.dot(_split3(jnp.concatenate([a_cum, dt_full], axis=0)), e_ref[...], preferred_element_type=F32)
    acum_x = both[:q]
    dt_x = both[q:]
    alast_x = acum_x[q - 1:q, :]
    decay_in = jnp.exp(acum_x)
    decay_out = jnp.exp(alast_x - acum_x)
    chunk_decay = jnp.exp(alast_x)

    xdt = xs * dt_x
    xdt_b = xdt.astype(BF16)
    xw_b = (xdt * decay_out).astype(BF16)

    y_diag, y_off = [], []
    for g in range(SSD_GROUPS):
        cm_g = cm[:, g * SSD_STATE:(g + 1) * SSD_STATE]
        bm_g = bm[:, g * SSD_STATE:(g + 1) * SSD_STATE]
        cm_b = cm_g.astype(BF16)
        cb = _nt_dot(cm_b, bm_g.astype(BF16))
        s_prev = state_ref[:, g * gw:(g + 1) * gw]
        y_off.append(jnp.dot(cm_b, s_prev.astype(BF16), preferred_element_type=F32))
        state_ref[:, g * gw:(g + 1) * gw] = (
            s_prev * chunk_decay[:, g * gw:(g + 1) * gw]
            + jnp.dot(bm_g.T.astype(BF16), xw_b[:, g * gw:(g + 1) * gw], preferred_element_type=F32))
        for j in range(hpg // 2):
            h0 = g * hpg + 2 * j
            xp = xdt_b[:, h0 * SSD_HEAD_DIM:(h0 + 2) * SSD_HEAD_DIM]
            pair = None
            for r in range(2):
                c = DT_LO + h0 + r
                seg = a_cum[:, c:c + 1] - a_cum_t[c:c + 1, :]
                decay = jnp.exp(jnp.where(causal, seg, -jnp.inf))
                m = (cb * decay).astype(BF16)
                keep = (lane < SSD_HEAD_DIM) if r == 0 else (lane >= SSD_HEAD_DIM)
                part = jnp.dot(m, jnp.where(keep, xp, jnp.zeros_like(xp)), preferred_element_type=F32)
                pair = part if pair is None else pair + part
            y_diag.append(pair)

    y = jnp.concatenate(y_diag, axis=1) + decay_in * jnp.concatenate(y_off, axis=1) + dskip_ref[...] * xs
    y = y * gz
    outs = []
    for g in range(SSD_GROUPS):
        yg = y[:, g * gw:(g + 1) * gw]
        outs.append(yg * lax.rsqrt(jnp.mean(yg * yg, axis=-1, keepdims=True) + EPS))
    y_ref[...] = (jnp.concatenate(outs, axis=1) * gssd_ref[...]).astype(y_ref.dtype)


def _key_to_f32(ks):
    return lax.bitcast_convert_type(ks ^ ((ks >> 31) & jnp.int32(0x7FFFFFFF)), F32)


COUNT_ACCS = 4


def _count_rows(m, cnt):
    cnt = list(cnt)
    for j, r in enumerate(range(0, m.shape[0], SUBLANES)):
        a = j % COUNT_ACCS
        cnt[a] = jnp.where(m[r:r + SUBLANES, :], cnt[a] + 1, cnt[a])
    return tuple(cnt)


def _dsa_kernel(q_ref, k_ref, vt_ref, qi_ref, kia_ref, kib_ref, sm_ref, o_ref,
                score_ref, bias_ref, thr_ref, mstar_ref, m_ref, acc_ref, logit_ref, cmax_ref, *, dm, seq, qb):
    kc = qb
    topk = dm["topk"]
    n_heads = dm["att_heads"]
    rep = n_heads // ATT_KV_HEADS
    i = pl.program_id(1)
    nkc = i + 1
    q_pos = i * qb + lax.broadcasted_iota(jnp.int32, (1, qb), 1)

    def key_pos(c):
        return c * kc + lax.broadcasted_iota(jnp.int32, (kc, 1), 0)

    def rows(c):
        return pl.ds(pl.multiple_of(c * kc, kc), kc)

    wi_t = sm_ref[...].T
    qi = qi_ref[...]

    def score_chunk(c):
        ka = kia_ref[rows(c), :]
        kb = kib_ref[rows(c), :]
        acc = jnp.zeros((kc, qb), F32)
        for j in range(IDX_HEADS // 2):
            qp = qi[:, j * LANES:(j + 1) * LANES]
            w0 = wi_t[WI_LO + 2 * j:WI_LO + 2 * j + 1, :]
            w1 = wi_t[WI_LO + 2 * j + 1:WI_LO + 2 * j + 2, :]
            acc = acc + w0 * jnp.maximum(_nt_dot(ka, qp), 0.0) + w1 * jnp.maximum(_nt_dot(kb, qp), 0.0)
        score_ref[rows(c), :] = jnp.where(key_pos(c) <= q_pos, acc, -jnp.inf)

    def pairwise(one_chunk):
        def two(j, carry):
            one_chunk(2 * j)
            one_chunk(2 * j + 1)
            return carry

        lax.fori_loop(0, nkc // 2, two, 0)

        @pl.when(nkc % 2 == 1)
        def _():
            one_chunk(nkc - 1)

    pairwise(score_chunk)

    def count(pred):
        def body(c, cnt):
            return _count_rows(pred(score_ref[rows(c), :], c), cnt)
        cnt = lax.fori_loop(0, nkc, body, (jnp.zeros((SUBLANES, qb), jnp.int32),) * COUNT_ACCS)
        return jnp.sum(functools.reduce(lambda a, b: a + b, cnt), axis=0, keepdims=True)

    def bit_step(p, carry):
        res, n_res = carry
        cand = res | (jnp.int32(1) << (31 - p))
        cand_s = cand ^ jnp.int32(INT_MIN)
        cand_f = _key_to_f32(cand_s)
        n_cand = count(lambda sc, c: sc >= cand_f)
        ok = (n_cand >= topk) | (cand_s <= jnp.int32(KEY_NEG_INF))
        return jnp.where(ok, cand, res), jnp.where(ok, n_cand, n_res)

    zeros_q = jnp.zeros((1, qb), jnp.int32)
    res, n_ge = lax.fori_loop(0, 32, bit_step, (zeros_q, zeros_q))
    thr_s = res ^ jnp.int32(INT_MIN)
    thr0 = _key_to_f32(thr_s)
    thr_ref[...] = thr0
    mstar_ref[...] = jnp.full((1, qb), seq, jnp.int32)

    @pl.when(jnp.max(n_ge) > topk)
    def _():
        def refine(_, lh):
            lo, hi = lh
            mid = lo + 0.5 * (hi - lo)
            ok = count(lambda sc, c: sc >= mid) >= topk
            return jnp.where(ok, mid, lo), jnp.where(ok, hi, mid)

        thr, _ = lax.fori_loop(0, REFINE_STEPS, refine, (thr0, _key_to_f32(thr_s + 1)))
        thr_ref[...] = thr
        need = topk - count(lambda sc, c: sc > thr)
        nbits = max(1, int(np.ceil(np.log2(seq))))

        def idx_step(p, res):
            cand = res | (jnp.int32(1) << (nbits - 1 - p))
            n_eq_before = count(lambda sc, c: (sc == thr) & (key_pos(c) < cand))
            return jnp.where(n_eq_before < need, cand, res)

        mstar_ref[...] = lax.fori_loop(0, nbits, idx_step, jnp.zeros((1, qb), jnp.int32))

    thr = thr_ref[...]
    mstar = mstar_ref[...]

    m_ref[...] = jnp.full(m_ref.shape, NEG_BIG, F32)
    acc_ref[...] = jnp.zeros(acc_ref.shape, F32)

    ones_rows = jnp.ones((SUM_ROWS, kc), BF16)

    def logits_phase(c):
        slot = (c % 2) * n_heads
        sc = score_ref[rows(c), :]
        kp = key_pos(c)
        sel = ((sc > thr) | ((sc == thr) & (kp <= mstar))) & (kp <= q_pos)
        bias_ref[rows(c), :] = jnp.where(sel, 0.0, NEG_BIG)
        for h in range(n_heads):
            g = h // rep
            kch = k_ref[rows(c), g * ATT_HEAD_DIM:(g + 1) * ATT_HEAD_DIM]
            logit = _nt_dot(kch, q_ref[:, h * ATT_HEAD_DIM:(h + 1) * ATT_HEAD_DIM]) + bias_ref[rows(c), :]
            logit_ref[slot + h] = logit
            cmax_ref[slot + h, 0:1, :] = jnp.max(logit, axis=0, keepdims=True)

    def softmax_phase(c):
        slot = (c % 2) * n_heads
        for h in range(n_heads):
            g = h // rep
            vt = jnp.concatenate([vt_ref[c, g * ATT_HEAD_DIM:(g + 1) * ATT_HEAD_DIM, :], ones_rows], axis=0)
            m_old = m_ref[h:h + 1, :]
            m_new = jnp.maximum(m_old, cmax_ref[slot + h, 0:1, :])
            p = jnp.exp2(logit_ref[slot + h] - m_new)
            m_ref[h:h + 1, :] = m_new
            acc_ref[h] = (acc_ref[h] * jnp.exp2(m_old - m_new)
                          + jnp.dot(vt, p.astype(BF16), preferred_element_type=F32))

    def att_chunk(c):
        logits_phase(c)
        softmax_phase(c)

    pairwise(att_chunk)
    for h in range(n_heads):
        acc = acc_ref[h]
        o = acc[:ATT_HEAD_DIM, :] / acc[ATT_HEAD_DIM:ATT_HEAD_DIM + 1, :]
        o_ref[:, h * ATT_HEAD_DIM:(h + 1) * ATT_HEAD_DIM] = o.T.astype(o_ref.dtype)


def _dsa(q, k, vt, qi, kia, kib, sm, *, dm, batch, seq, qb):
    t = q.shape[0]
    nqb = seq // qb
    n_kv = dm["n_kv"]
    assert qb >= dm["topk"] and seq % qb == 0
    qrow = lambda width: pl.BlockSpec((qb, width), lambda b, i: (b * nqb + i, 0))
    brow = lambda width: pl.BlockSpec((seq, width), lambda b, i: (b, 0))
    return pl.pallas_call(
        functools.partial(_dsa_kernel, dm=dm, seq=seq, qb=qb),
        grid=(batch, nqb),
        in_specs=[qrow(dm["n_q"]), brow(n_kv),
                  pl.BlockSpec((nqb, n_kv, qb), lambda b, i: (b, 0, 0)),
                  qrow(dm["n_qi"]), brow(LANES), brow(LANES), qrow(LANES)],
        out_specs=qrow(dm["n_q"]),
        out_shape=jax.ShapeDtypeStruct((t, dm["n_q"]), BF16),
        scratch_shapes=[pltpu.VMEM((seq, qb), F32), pltpu.VMEM((seq, qb), F32),
                        pltpu.VMEM((1, qb), F32), pltpu.VMEM((1, qb), jnp.int32),
                        pltpu.VMEM((dm["att_heads"], qb), F32),
                        pltpu.VMEM((dm["att_heads"], ATT_HEAD_DIM + SUM_ROWS, qb), F32),
                        pltpu.VMEM((2 * dm["att_heads"], qb, qb), F32),
                        pltpu.VMEM((2 * dm["att_heads"], SUBLANES, qb), F32)],
        compiler_params=pltpu.CompilerParams(dimension_semantics=("arbitrary", "arbitrary"), vmem_limit_bytes=VMEM_LIMIT),
        name="dsa",
    )(q, k, vt, qi, kia, kib, sm)


def _out_proj_kernel(x_ref, ya_ref, yb_ref, w_ref, o_ref):
    na = ya_ref.shape[1]
    o_ref[...] = (x_ref[...]
                  + jnp.dot(ya_ref[...], w_ref[:na, :], preferred_element_type=F32)
                  + jnp.dot(yb_ref[...], w_ref[na:, :], preferred_element_type=F32))


def _out_proj(x2, y_ssd, y_att, w_out, *, layer, tm):
    t, d = x2.shape
    row = lambda width: pl.BlockSpec((tm, width), lambda i: (i, 0))
    return pl.pallas_call(
        _out_proj_kernel,
        grid=(t // tm,),
        in_specs=[row(d), row(y_ssd.shape[1]), row(y_att.shape[1]),
                  pl.BlockSpec((None,) + w_out.shape[1:], lambda i: (layer, 0, 0), pipeline_mode=pl.Buffered(1))],
        out_specs=row(d),
        out_shape=jax.ShapeDtypeStruct((t, d), F32),
        compiler_params=pltpu.CompilerParams(dimension_semantics=("arbitrary",), vmem_limit_bytes=VMEM_LIMIT),
        name="out_proj",
    )(x2, y_ssd, y_att, w_out)


def _mlp_kernel(x_ref, g_ref, wu_ref, wd_ref, o_ref, xn_ref, acc_ref):
    f = pl.program_id(1)

    @pl.when(f == 0)
    def _():
        x = x_ref[...]
        ms = jnp.mean(x * x, axis=-1, keepdims=True)
        xn_ref[...] = (x * lax.rsqrt(ms + EPS) * g_ref[...]).astype(BF16)
        acc_ref[...] = jnp.zeros_like(acc_ref)

    u = jnp.maximum(jnp.dot(xn_ref[...], wu_ref[...], preferred_element_type=F32), 0.0)
    acc_ref[...] += jnp.dot((u * u).astype(BF16), wd_ref[...], preferred_element_type=F32)

    @pl.when(f == pl.num_programs(1) - 1)
    def _():
        o_ref[...] = x_ref[...] + acc_ref[...]


def _mlp(x2, g_mlp, w_up, w_down, *, layer, tm, tf):
    t, d = x2.shape
    d_ff = w_up.shape[2]
    return pl.pallas_call(
        _mlp_kernel,
        grid=(t // tm, d_ff // tf),
        in_specs=[pl.BlockSpec((tm, d), lambda i, f: (i, 0)),
                  pl.BlockSpec((1, d), lambda i, f: (0, 0)),
                  pl.BlockSpec((None, d, tf), lambda i, f: (layer, 0, f)),
                  pl.BlockSpec((None, tf, d), lambda i, f: (layer, f, 0))],
        out_specs=pl.BlockSpec((tm, d), lambda i, f: (i, 0)),
        out_shape=jax.ShapeDtypeStruct((t, d), F32),
        scratch_shapes=[pltpu.VMEM((tm, d), BF16), pltpu.VMEM((tm, d), F32)],
        compiler_params=pltpu.CompilerParams(dimension_semantics=("arbitrary", "arbitrary"), vmem_limit_bytes=VMEM_LIMIT),
        name="mlp",
    )(x2, g_mlp, w_up, w_down)


def _pack_w_kernel(w_ref, tail_ref, o_ref, *, dm):
    z_w, cc = dm["ssd_w"], dm["conv_ch"]
    rest_lo = z_w + cc + dm["ssd_heads"]
    rest_w = dm["n_q"] + 2 * dm["n_kv"] + dm["n_qi"]
    base = rest_lo // LANES * LANES
    o_ref[:, 0:cc] = w_ref[:, z_w:z_w + cc].astype(BF16)
    o_ref[:, cc:cc + 2 * LANES] = tail_ref[...].astype(BF16)
    o_ref[:, cc + 2 * LANES:cc + 2 * LANES + z_w] = w_ref[:, 0:z_w].astype(BF16)
    rest = w_ref[:, base:][:, rest_lo - base:rest_lo - base + rest_w]
    o_ref[:, cc + 2 * LANES + z_w:] = rest.astype(BF16)


def _pack_w_in(w_in, dm):
    depth, d, n_in = w_in.shape
    sizes = (dm["ssd_w"], dm["conv_ch"], dm["ssd_heads"], dm["n_q"], dm["n_kv"], dm["n_kv"], dm["n_qi"],
             IDX_HEAD_DIM, IDX_HEADS)
    assert sum(sizes) == n_in
    lo = int(np.sum(sizes[:2]))
    dt = w_in[:, :, lo:lo + dm["ssd_heads"]]
    ki = w_in[:, :, n_in - IDX_HEADS - IDX_HEAD_DIM:n_in - IDX_HEADS]
    wi = w_in[:, :, n_in - IDX_HEADS:]
    pad = jnp.zeros((depth, d, LANES - IDX_HEADS - dm["ssd_heads"]), w_in.dtype)
    tail = jnp.concatenate([ki, ki, wi, dt, pad], axis=2)
    nw = n_in - dm["ssd_heads"] - IDX_HEAD_DIM - IDX_HEADS + 2 * LANES
    tr = min(256, d)
    assert d % tr == 0
    return pl.pallas_call(
        functools.partial(_pack_w_kernel, dm=dm),
        grid=(depth, d // tr),
        in_specs=[pl.BlockSpec((None, tr, n_in), lambda l, r: (l, r, 0)),
                  pl.BlockSpec((None, tr, 2 * LANES), lambda l, r: (l, r, 0))],
        out_specs=pl.BlockSpec((None, tr, nw), lambda l, r: (l, r, 0)),
        out_shape=jax.ShapeDtypeStruct((depth, d, nw), BF16),
        compiler_params=pltpu.CompilerParams(dimension_semantics=("arbitrary", "arbitrary"), vmem_limit_bytes=VMEM_LIMIT),
        name="pack_w_in",
    )(w_in, tail)


def _expand_matrix(dm):
    e = np.zeros((LANES, dm["ssd_w"]), np.float32)
    for h in range(dm["ssd_heads"]):
        e[DT_LO + h, h * SSD_HEAD_DIM:(h + 1) * SSD_HEAD_DIM] = 1.0
    return jnp.asarray(np.concatenate([e, e, e], axis=0)).astype(BF16)


def _lane_pad(v, lo):
    return jnp.zeros((1, LANES), F32).at[0, lo:lo + v.shape[0]].set(v)


def _tiles(tokens, seq, d_ff):
    qb = 2 * LANES
    tm = min(4 * LANES, tokens)
    tf = min(8 * LANES, d_ff)
    assert seq % qb == 0 and tokens % tm == 0 and d_ff % tf == 0
    return qb, tm, tf


def kernel(x, g_mix, w_in, conv_w, conv_b, dt_bias, a_log, d_skip, g_ssd, g_q, g_k, g_kidx, w_out, g_mlp, w_up, w_down):
    batch, seq, d = x.shape
    depth = w_in.shape[0]
    dm = _dims(d, seq)
    assert dm["ssd_heads"] + IDX_HEADS <= LANES and (dm["ssd_heads"] // SSD_GROUPS) % 2 == 0
    assert seq % SSD_CHUNK == 0 and dm["att_heads"] % ATT_KV_HEADS == 0
    qb, tm, tf = _tiles(batch * seq, seq, w_up.shape[2])
    e_mat = _expand_matrix(dm)
    w_all = _pack_w_in(w_in, dm)
    w_out, w_up, w_down = (w.astype(BF16) for w in (w_out, w_up, w_down))

    x2 = x.reshape(batch * seq, d)
    for i in range(depth):
        y_ssd, q, k, vt, qi, kia, kib, sm = _in_proj(
            x2, g_mix[i][None, :], w_all, g_q[i][None, :], g_k[i][None, :],
            jnp.concatenate([g_kidx[i], g_kidx[i]])[None, :], _lane_pad(dt_bias[i], DT_LO),
            conv_w[i], conv_b[i][None, :], _lane_pad(a_log[i], DT_LO),
            jnp.repeat(d_skip[i], SSD_HEAD_DIM)[None, :], g_ssd[i][None, :], e_mat,
            layer=i, dm=dm, tm=qb, seq=seq)
        y_att = _dsa(q, k, vt, qi, kia, kib, sm, dm=dm, batch=batch, seq=seq, qb=qb)
        x2 = _out_proj(x2, y_ssd, y_att, w_out, layer=i, tm=tm)
        x2 = _mlp(x2, g_mlp[i][None, :], w_up, w_down, layer=i, tm=tm, tf=tf)
    return x2.reshape(batch, seq, d)
```
